```python
import jax, jax.numpy as jnp
from jax import lax
import numpy as np

D_MODEL = 2048
BATCH = 1
SEQ = 16384
DEPTH = 1

MEM_LEN = 256
D_FF = 5632
FFN_RES = 0.5
D_MIX = D_MODEL
D_CONV_A = D_MIX // 2
D_CONV_B = D_MIX - D_CONV_A
CH_GROUP = 64
N_GROUPS_A = D_CONV_A // CH_GROUP
N_GROUPS_B = D_CONV_B // CH_GROUP
K_SHORT = 3
K_CONF = 31
D_IN_PROJ = 3 * D_CONV_A + 2 * D_CONV_B
N_MEM_HEADS = 4
MEM_HEAD_DIM = D_MODEL // N_MEM_HEADS
RMS_EPS = 1e-6
LN_EPS = 1e-5

kernel_name = "hybrid_shortconv_conformer_macaron_block"


def rmsnorm(x, g):
    xf = x.astype(jnp.float32)
    y = xf * lax.rsqrt(jnp.mean(xf * xf, axis=-1, keepdims=True) + RMS_EPS)
    return (y * g.astype(jnp.float32)).astype(x.dtype)


def layernorm(x, g, b):
    xf = x.astype(jnp.float32)
    mu = jnp.mean(xf, axis=-1, keepdims=True)
    var = jnp.mean(jnp.square(xf - mu), axis=-1, keepdims=True)
    y = (xf - mu) * lax.rsqrt(var + LN_EPS)
    return (y * g.astype(jnp.float32) + b.astype(jnp.float32)).astype(x.dtype)


def causal_depthwise_conv(x, w):
    k, c = w.shape
    return lax.conv_general_dilated(
        x, w[:, None, :].astype(x.dtype), window_strides=(1,), padding=((k - 1, 0),),
        dimension_numbers=("NWC", "WIO", "NWC"), feature_group_count=c)


def swiglu_ffn(u, w_gate_up, w_down):
    gu = jnp.einsum("bsd,df->bsf", u, w_gate_up)
    g, up = jnp.split(gu, 2, axis=-1)
    return jnp.einsum("bsf,fd->bsd", jax.nn.silu(g) * up, w_down)


def parallel_conv_mixer(u, w_in, conv_a_w, conv_b_w, conv_b_bias, ln_b_gain, ln_b_bias, w_out):
    z = jnp.einsum("bsd,de->bse", u, w_in)
    b_gate, c_gate, xa, vb, gb = jnp.split(
        z, [D_CONV_A, 2 * D_CONV_A, 3 * D_CONV_A, 3 * D_CONV_A + D_CONV_B], axis=-1)
    ya = b_gate * causal_depthwise_conv(c_gate * xa, conv_a_w)
    hb = vb * jax.nn.sigmoid(gb)
    hb = causal_depthwise_conv(hb, conv_b_w) + conv_b_bias.astype(hb.dtype)
    yb = jax.nn.silu(layernorm(hb, ln_b_gain, ln_b_bias))
    y = jnp.concatenate([ya, yb], axis=-1)
    return jnp.einsum("bse,ed->bsd", y, w_out)


def memory_cross_attention(u, mem_n, w_q, w_kv, w_o):
    b, s, _ = u.shape
    m = mem_n.shape[1]
    q = jnp.einsum("bsd,de->bse", u, w_q).reshape(b, s, N_MEM_HEADS, MEM_HEAD_DIM)
    kv = jnp.einsum("bmd,de->bme", mem_n, w_kv)
    k, v = jnp.split(kv, 2, axis=-1)
    k = k.reshape(b, m, N_MEM_HEADS, MEM_HEAD_DIM)
    v = v.reshape(b, m, N_MEM_HEADS, MEM_HEAD_DIM)
    scores = jnp.einsum("bshk,bmhk->bhsm", q, k).astype(jnp.float32) * (MEM_HEAD_DIM ** -0.5)
    p = jax.nn.softmax(scores, axis=-1).astype(v.dtype)
    o = jnp.einsum("bhsm,bmhk->bshk", p, v).reshape(b, s, N_MEM_HEADS * MEM_HEAD_DIM)
    return jnp.einsum("bse,ed->bsd", o, w_o)


def setup_inputs(seed: int = 0) -> dict:
    key = jax.random.key(seed)
    ks = jax.random.split(key, 24)
    f32 = jnp.float32

    def nrm(k, shape, fan_in):
        return jax.random.normal(k, shape, f32) * (fan_in ** -0.5)

    def gain(k, shape):
        return 1.0 + 0.02 * jax.random.normal(k, shape, f32)

    L, D = DEPTH, D_MODEL
    return {
        "x": jax.random.normal(ks[0], (BATCH, SEQ, D), f32),
        "mem": jax.random.normal(ks[1], (BATCH, MEM_LEN, D), f32),
        "ffn1_norm": gain(ks[2], (L, D)),
        "ffn1_w_gate_up": nrm(ks[3], (L, D, 2 * D_FF), D),
        "ffn1_w_down": nrm(ks[4], (L, D_FF, D), D_FF),
        "mix_norm": gain(ks[5], (L, D)),
        "w_mix_in": nrm(ks[6], (L, D, D_IN_PROJ), D),
        "conv_a_w": nrm(ks[7], (L, K_SHORT, D_CONV_A), K_SHORT),
        "conv_b_w": nrm(ks[8], (L, K_CONF, D_CONV_B), K_CONF),
        "conv_b_bias": 0.02 * jax.random.normal(ks[9], (L, D_CONV_B), f32),
        "ln_b_gain": gain(ks[10], (L, D_CONV_B)),
        "ln_b_bias": 0.02 * jax.random.normal(ks[11], (L, D_CONV_B), f32),
        "w_mix_out": nrm(ks[12], (L, D_MIX, D), D_MIX),
        "mem_q_norm": gain(ks[13], (L, D)),
        "mem_kv_norm": gain(ks[14], (L, D)),
        "w_mem_q": nrm(ks[15], (L, D, N_MEM_HEADS * MEM_HEAD_DIM), D),
        "w_mem_kv": nrm(ks[16], (L, D, 2 * N_MEM_HEADS * MEM_HEAD_DIM), D),
        "w_mem_o": nrm(ks[17], (L, N_MEM_HEADS * MEM_HEAD_DIM, D), N_MEM_HEADS * MEM_HEAD_DIM),
        "ffn2_norm": gain(ks[18], (L, D)),
        "ffn2_w_gate_up": nrm(ks[19], (L, D, 2 * D_FF), D),
        "ffn2_w_down": nrm(ks[20], (L, D_FF, D), D_FF),
        "final_norm": gain(ks[21], (D,)),
    }


def reference(x, mem, ffn1_norm, ffn1_w_gate_up, ffn1_w_down, mix_norm, w_mix_in, conv_a_w,
              conv_b_w, conv_b_bias, ln_b_gain, ln_b_bias, w_mix_out, mem_q_norm, mem_kv_norm,
              w_mem_q, w_mem_kv, w_mem_o, ffn2_norm, ffn2_w_gate_up, ffn2_w_down, final_norm):
    h = x
    for l in range(DEPTH):
        h = h + FFN_RES * swiglu_ffn(rmsnorm(h, ffn1_norm[l]), ffn1_w_gate_up[l], ffn1_w_down[l])
        h = h + parallel_conv_mixer(rmsnorm(h, mix_norm[l]), w_mix_in[l], conv_a_w[l], conv_b_w[l],
                                    conv_b_bias[l], ln_b_gain[l], ln_b_bias[l], w_mix_out[l])
        h = h + memory_cross_attention(rmsnorm(h, mem_q_norm[l]), rmsnorm(mem, mem_kv_norm[l]),
                                       w_mem_q[l], w_mem_kv[l], w_mem_o[l])
        h = h + FFN_RES * swiglu_ffn(rmsnorm(h, ffn2_norm[l]), ffn2_w_gate_up[l], ffn2_w_down[l])
    return rmsnorm(h, final_norm)
```

```python
import functools

import jax
import jax.numpy as jnp
from jax import lax
from jax.experimental import pallas as pl
from jax.experimental.pallas import tpu as pltpu

D_MODEL = 2048
SEQ = 16384
MEM_LEN = 256
D_FF = 5632
FFN_RES = 0.5
D_CONV = 1024
K_SHORT = 3
K_CONF = 31
N_MEM_HEADS = 4
MEM_HEAD_DIM = 512
RMS_EPS = 1e-6
LN_EPS = 1e-5

SUBLANES = 8
LANES = 128
VMEM_LIMIT_BYTES = 60 * 1024 * 1024

FFN_ROWS = 512
FFN_COLS = 512
MIX_ROWS = 256
ATT_ROWS = 512
HALO_A = SUBLANES
HALO_B = 32
CONV_ROW_CHUNK = 64

f32 = jnp.float32
bf16 = jnp.bfloat16


def _rmsnorm(x, g):
    return x * lax.rsqrt(jnp.mean(x * x, axis=-1, keepdims=True) + RMS_EPS) * g


def _resident(shape):
    return pl.BlockSpec(shape, lambda *_: (0,) * len(shape), pipeline_mode=pl.Buffered(1))


def _ffn_kernel(x_ref, g_ref, wg_ref, wu_ref, wd_ref, fin_ref, o_ref, u_ref, *, final_norm):
    j = pl.program_id(1)

    @pl.when(j == 0)
    def _():
        u_ref[...] = _rmsnorm(x_ref[...], g_ref[...]).astype(bf16)

    u = u_ref[...]
    gate = jnp.dot(u, wg_ref[...], preferred_element_type=f32)
    up = jnp.dot(u, wu_ref[...], preferred_element_type=f32)
    act = (gate * jax.nn.sigmoid(gate) * up).astype(bf16)
    down = jnp.dot(act, wd_ref[...], preferred_element_type=f32)

    @pl.when(j == 0)
    def _():
        o_ref[...] = down

    @pl.when(j > 0)
    def _():
        o_ref[...] += down

    @pl.when(j == pl.num_programs(1) - 1)
    def _():
        h = x_ref[...] + FFN_RES * o_ref[...]
        if final_norm:
            h = _rmsnorm(h, fin_ref[...])
        o_ref[...] = h


def _ffn(h, norm_g, w_gate_up, w_down, final_g, *, final_norm):
    n_col = D_FF // FFN_COLS
    return pl.pallas_call(
        functools.partial(_ffn_kernel, final_norm=final_norm),
        grid=(SEQ // FFN_ROWS, n_col),
        in_specs=[
            pl.BlockSpec((FFN_ROWS, D_MODEL), lambda i, j: (i, 0)),
            pl.BlockSpec((1, D_MODEL), lambda i, j: (0, 0)),
            pl.BlockSpec((D_MODEL, FFN_COLS), lambda i, j: (0, j)),
            pl.BlockSpec((D_MODEL, FFN_COLS), lambda i, j: (0, j + n_col)),
            pl.BlockSpec((FFN_COLS, D_MODEL), lambda i, j: (j, 0)),
            pl.BlockSpec((1, D_MODEL), lambda i, j: (0, 0)),
        ],
        out_specs=pl.BlockSpec((FFN_ROWS, D_MODEL), lambda i, j: (i, 0)),
        out_shape=jax.ShapeDtypeStruct((SEQ, D_MODEL), f32),
        scratch_shapes=[pltpu.VMEM((FFN_ROWS, D_MODEL), bf16)],
        compiler_params=pltpu.CompilerParams(
            dimension_semantics=("arbitrary", "arbitrary"),
            vmem_limit_bytes=VMEM_LIMIT_BYTES),
        name="ffn_final" if final_norm else "ffn",
    )(h, norm_g, w_gate_up, w_gate_up, w_down, final_g)


def _mixer_kernel(h_ref, g_ref, win_ref, caw_ref, cbw_ref, cbb_ref, lng_ref, lnb_ref, wout_ref,
                  o_ref, ca_ref, hb_ref, cv_ref, y_ref):
    rows = h_ref.shape[0]

    @pl.when(pl.program_id(0) == 0)
    def _():
        ca_ref[0:HALO_A, :] = jnp.zeros((HALO_A, D_CONV), f32)
        hb_ref[0:HALO_B, :] = jnp.zeros((HALO_B, D_CONV), f32)

    x = h_ref[...]
    u = _rmsnorm(x, g_ref[...]).astype(bf16)

    def proj(k):
        return jnp.dot(u, win_ref[:, k * D_CONV:(k + 1) * D_CONV], preferred_element_type=f32)

    ca_ref[HALO_A:HALO_A + rows, :] = proj(1) * proj(2)
    conv_a = caw_ref[K_SHORT - 1:K_SHORT, :] * ca_ref[HALO_A:HALO_A + rows, :]
    for j in range(K_SHORT - 1):
        shift = K_SHORT - 1 - j
        conv_a += caw_ref[j:j + 1, :] * ca_ref[HALO_A - shift:HALO_A - shift + rows, :]
    y_ref[:, 0:D_CONV] = (proj(0) * conv_a).astype(bf16)

    hb_ref[HALO_B:HALO_B + rows, :] = proj(3) * jax.nn.sigmoid(proj(4))
    for c in range(D_CONV // LANES):
        cols = slice(c * LANES, (c + 1) * LANES)
        taps = [cbw_ref[j:j + 1, cols] for j in range(K_CONF)]
        bias = cbb_ref[:, cols]
        for r in range(rows // CONV_ROW_CHUNK):
            r0 = r * CONV_ROW_CHUNK
            acc = jnp.broadcast_to(bias, (CONV_ROW_CHUNK, LANES))
            for j in range(K_CONF):
                start = HALO_B + r0 - (K_CONF - 1) + j
                acc = acc + taps[j] * hb_ref[start:start + CONV_ROW_CHUNK, cols]
            cv_ref[r0:r0 + CONV_ROW_CHUNK, cols] = acc
    cv = cv_ref[...]
    mu = jnp.mean(cv, axis=-1, keepdims=True)
    cen = cv - mu
    var = jnp.mean(cen * cen, axis=-1, keepdims=True)
    ln = cen * lax.rsqrt(var + LN_EPS) * lng_ref[...] + lnb_ref[...]
    y_ref[:, D_CONV:2 * D_CONV] = (ln * jax.nn.sigmoid(ln)).astype(bf16)

    o_ref[...] = x + jnp.dot(y_ref[...], wout_ref[...], preferred_element_type=f32)

    ca_ref[0:HALO_A, :] = ca_ref[rows:rows + HALO_A, :]
    hb_ref[0:HALO_B, :] = hb_ref[rows:rows + HALO_B, :]


def _mixer(h, norm_g, w_in, conv_a_w, conv_b_w, conv_b_bias, ln_g, ln_b, w_out):
    row = lambda i: (i, 0)
    return pl.pallas_call(
        _mixer_kernel,
        grid=(SEQ // MIX_ROWS,),
        in_specs=[
            pl.BlockSpec((MIX_ROWS, D_MODEL), row),
            _resident((1, D_MODEL)),
            _resident(w_in.shape),
            _resident(conv_a_w.shape),
            _resident(conv_b_w.shape),
            _resident((1, D_CONV)),
            _resident((1, D_CONV)),
            _resident((1, D_CONV)),
            _resident(w_out.shape),
        ],
        out_specs=pl.BlockSpec((MIX_ROWS, D_MODEL), row),
        out_shape=jax.ShapeDtypeStruct((SEQ, D_MODEL), f32),
        scratch_shapes=[
            pltpu.VMEM((HALO_A + MIX_ROWS, D_CONV), f32),
            pltpu.VMEM((HALO_B + MIX_ROWS, D_CONV), f32),
            pltpu.VMEM((MIX_ROWS, D_CONV), f32),
            pltpu.VMEM((MIX_ROWS, 2 * D_CONV), bf16),
        ],
        compiler_params=pltpu.CompilerParams(
            dimension_semantics=("arbitrary",),
            vmem_limit_bytes=VMEM_LIMIT_BYTES),
        name="conv_mixer",
    )(h, norm_g, w_in, conv_a_w, conv_b_w, conv_b_bias, ln_g, ln_b, w_out)


def _mem_kv_kernel(mem_ref, g_ref, wkv_ref, kt_ref, v_ref):
    mem_n = _rmsnorm(mem_ref[...], g_ref[...]).astype(bf16)
    kv = jnp.dot(mem_n, wkv_ref[...], preferred_element_type=f32)
    width = N_MEM_HEADS * MEM_HEAD_DIM
    kt_ref[...] = kv[:, :width].T.astype(bf16)
    v_ref[...] = kv[:, width:].astype(bf16)


def _mem_kv(mem, norm_g, w_kv):
    width = N_MEM_HEADS * MEM_HEAD_DIM
    return pl.pallas_call(
        _mem_kv_kernel,
        out_shape=(jax.ShapeDtypeStruct((width, MEM_LEN), bf16),
                   jax.ShapeDtypeStruct((MEM_LEN, width), bf16)),
        compiler_params=pltpu.CompilerParams(vmem_limit_bytes=VMEM_LIMIT_BYTES),
        name="mem_kv",
    )(mem, norm_g, w_kv)


def _attn_kernel(h_ref, g_ref, wq_ref, kt_ref, v_ref, wo_ref, o_ref, att_ref):
    x = h_ref[...]
    u = _rmsnorm(x, g_ref[...]).astype(bf16)
    q = jnp.dot(u, wq_ref[...], preferred_element_type=f32).astype(bf16)
    for hd in range(N_MEM_HEADS):
        cols = slice(hd * MEM_HEAD_DIM, (hd + 1) * MEM_HEAD_DIM)
        s = jnp.dot(q[:, cols], kt_ref[cols, :], preferred_element_type=f32) * (MEM_HEAD_DIM ** -0.5)
        e = jnp.exp(s - jnp.max(s, axis=-1, keepdims=True))
        p = (e / jnp.sum(e, axis=-1, keepdims=True)).astype(bf16)
        att_ref[:, cols] = jnp.dot(p, v_ref[:, cols], preferred_element_type=f32).astype(bf16)
    o_ref[...] = x + jnp.dot(att_ref[...], wo_ref[...], preferred_element_type=f32)


def _attn(h, norm_g, w_q, k_t, v, w_o):
    row = lambda i: (i, 0)
    return pl.pallas_call(
        _attn_kernel,
        grid=(SEQ // ATT_ROWS,),
        in_specs=[
            pl.BlockSpec((ATT_ROWS, D_MODEL), row),
            _resident((1, D_MODEL)),
            _resident(w_q.shape),
            _resident(k_t.shape),
            _resident(v.shape),
            _resident(w_o.shape),
        ],
        out_specs=pl.BlockSpec((ATT_ROWS, D_MODEL), row),
        out_shape=jax.ShapeDtypeStruct((SEQ, D_MODEL), f32),
        scratch_shapes=[pltpu.VMEM((ATT_ROWS, D_MODEL), bf16)],
        compiler_params=pltpu.CompilerParams(
            dimension_semantics=("arbitrary",),
            vmem_limit_bytes=VMEM_LIMIT_BYTES),
        name="mem_attn",
    )(h, norm_g, w_q, k_t, v, w_o)


def kernel(x, mem, ffn1_norm, ffn1_w_gate_up, ffn1_w_down, mix_norm, w_mix_in, conv_a_w, conv_b_w,
           conv_b_bias, ln_b_gain, ln_b_bias, w_mix_out, mem_q_norm, mem_kv_norm, w_mem_q, w_mem_kv,
           w_mem_o, ffn2_norm, ffn2_w_gate_up, ffn2_w_down, final_norm):
    assert x.shape == (1, SEQ, D_MODEL) and mem.shape == (1, MEM_LEN, D_MODEL)
    assert ffn1_norm.shape[0] == 1, "one layer"
    h = x[0]
    final_g = final_norm[None, :]

    h = _ffn(h, ffn1_norm, ffn1_w_gate_up[0].astype(bf16), ffn1_w_down[0].astype(bf16), final_g,
             final_norm=False)
    h = _mixer(h, mix_norm, w_mix_in[0].astype(bf16), conv_a_w[0], conv_b_w[0], conv_b_bias,
               ln_b_gain, ln_b_bias, w_mix_out[0].astype(bf16))
    k_t, v = _mem_kv(mem[0], mem_kv_norm, w_mem_kv[0].astype(bf16))
    h = _attn(h, mem_q_norm, w_mem_q[0].astype(bf16), k_t, v, w_mem_o[0].astype(bf16))
    h = _ffn(h, ffn2_norm, ffn2_w_gate_up[0].astype(bf16), ffn2_w_down[0].astype(bf16), final_g,
             final_norm=True)
    return h[None]
```

```python
import functools

import jax
import jax.numpy as jnp
from jax import lax
from jax.experimental import pallas as pl
from jax.experimental.pallas import tpu as pltpu

D_MODEL = 2048
SEQ = 16384
MEM_LEN = 256
D_FF = 5632
FFN_RES = 0.5
D_CONV = 1024
K_SHORT = 3
K_CONF = 31
N_MEM_HEADS = 4
MEM_HEAD_DIM = 512
RMS_EPS = 1e-6
LN_EPS = 1e-5

SUBLANES = 8
LANES = 128
MXU_COLS = 512
VMEM_LIMIT_BYTES = 60 * 1024 * 1024

FFN_ROWS = 1024
FFN_COLS = 512
MIX_ROWS = 256
ATT_ROWS = 1024
HALO_A = SUBLANES
HALO_B = 32
CONV_ROW_CHUNK = 64

f32 = jnp.float32
bf16 = jnp.bfloat16


def _rmsnorm(x, g):
    return x * lax.rsqrt(jnp.mean(x * x, axis=-1, keepdims=True) + RMS_EPS) * g


def _resident(shape):
    return pl.BlockSpec(shape, lambda *_: (0,) * len(shape), pipeline_mode=pl.Buffered(1))


def _ffn_kernel(x_ref, g_ref, wg_ref, wu_ref, wd_ref, fin_ref, o_ref, u_ref, *, final_norm):
    j = pl.program_id(1)

    @pl.when(j == 0)
    def _():
        x = x_ref[...]
        u_ref[...] = _rmsnorm(x, g_ref[...]).astype(bf16)
        o_ref[...] = x

    u = u_ref[...]
    gate = jnp.dot(u, wg_ref[...], preferred_element_type=f32)
    up = jnp.dot(u, wu_ref[...], preferred_element_type=f32)
    act = (gate * jax.nn.sigmoid(gate) * (FFN_RES * up)).astype(bf16)
    o_ref[...] += jnp.dot(act, wd_ref[...], preferred_element_type=f32)

    if final_norm:
        @pl.when(j == pl.num_programs(1) - 1)
        def _():
            o_ref[...] = _rmsnorm(o_ref[...], fin_ref[...])


def _ffn(h, norm_g, w_gate_up, w_down, final_g, *, final_norm):
    n_col = D_FF // FFN_COLS
    return pl.pallas_call(
        functools.partial(_ffn_kernel, final_norm=final_norm),
        grid=(SEQ // FFN_ROWS, n_col),
        in_specs=[
            pl.BlockSpec((FFN_ROWS, D_MODEL), lambda i, j: (i, 0)),
            pl.BlockSpec((1, D_MODEL), lambda i, j: (0, 0)),
            pl.BlockSpec((D_MODEL, FFN_COLS), lambda i, j: (0, j)),
            pl.BlockSpec((D_MODEL, FFN_COLS), lambda i, j: (0, j + n_col)),
            pl.BlockSpec((FFN_COLS, D_MODEL), lambda i, j: (j, 0)),
            pl.BlockSpec((1, D_MODEL), lambda i, j: (0, 0)),
        ],
        out_specs=pl.BlockSpec((FFN_ROWS, D_MODEL), lambda i, j: (i, 0)),
        out_shape=jax.ShapeDtypeStruct((SEQ, D_MODEL), f32),
        scratch_shapes=[pltpu.VMEM((FFN_ROWS, D_MODEL), bf16)],
        compiler_params=pltpu.CompilerParams(
            dimension_semantics=("arbitrary", "arbitrary"),
            vmem_limit_bytes=VMEM_LIMIT_BYTES),
        name="ffn_final" if final_norm else "ffn",
    )(h, norm_g, w_gate_up, w_gate_up, w_down, final_g)


def _mixer_kernel(h_ref, g_ref, win_ref, caw_ref, cbw_ref, cbb_ref, lng_ref, lnb_ref, wout_ref,
                  o_ref, u_ref, z_ref, ca_ref, hb_ref, sh_ref, cv_ref, y_ref):
    rows = h_ref.shape[0]

    @pl.when(pl.program_id(0) == 0)
    def _():
        ca_ref[0:HALO_A, :] = jnp.zeros((HALO_A, D_CONV), f32)
        hb_ref[0:HALO_B, :] = jnp.zeros((HALO_B, D_CONV), f32)

    x = h_ref[...]
    u_ref[...] = _rmsnorm(x, g_ref[...]).astype(bf16)

    def proj(col0):
        return jnp.dot(u_ref[...], win_ref[:, col0:col0 + MXU_COLS], preferred_element_type=f32)

    for c0 in range(0, D_CONV, MXU_COLS):
        hb_ref[HALO_B:HALO_B + rows, c0:c0 + MXU_COLS] = (
            proj(3 * D_CONV + c0) * jax.nn.sigmoid(proj(4 * D_CONV + c0)))
    for k in range(3):
        for c0 in range(0, D_CONV, MXU_COLS):
            z_ref[k, :, c0:c0 + MXU_COLS] = proj(k * D_CONV + c0)
    for r in range(1, SUBLANES):
        sh_ref[r - 1, SUBLANES:HALO_B + rows, :] = hb_ref[SUBLANES - r:HALO_B + rows - r, :]
    for c in range(D_CONV // LANES):
        cols = slice(c * LANES, (c + 1) * LANES)
        taps = [cbw_ref[j:j + 1, cols] for j in range(K_CONF)]
        bias = cbb_ref[:, cols]
        for rc in range(rows // CONV_ROW_CHUNK):
            r0 = HALO_B + rc * CONV_ROW_CHUNK
            acc = jnp.broadcast_to(bias, (CONV_ROW_CHUNK, LANES))
            for j in range(K_CONF):
                q, r = divmod(K_CONF - 1 - j, SUBLANES)
                start = r0 - q * SUBLANES
                if r == 0:
                    src = hb_ref[start:start + CONV_ROW_CHUNK, cols]
                else:
                    src = sh_ref[r - 1, start:start + CONV_ROW_CHUNK, cols]
                acc = acc + taps[j] * src
            cv_ref[rc * CONV_ROW_CHUNK:(rc + 1) * CONV_ROW_CHUNK, cols] = acc
    cv = cv_ref[...]
    mu = jnp.mean(cv, axis=-1, keepdims=True)
    cen = cv - mu
    var = jnp.mean(cen * cen, axis=-1, keepdims=True)
    ln = cen * lax.rsqrt(var + LN_EPS) * lng_ref[...] + lnb_ref[...]
    y_ref[:, D_CONV:2 * D_CONV] = (ln * jax.nn.sigmoid(ln)).astype(bf16)

    ca_ref[HALO_A:HALO_A + rows, :] = z_ref[1] * z_ref[2]
    conv_a = caw_ref[K_SHORT - 1:K_SHORT, :] * ca_ref[HALO_A:HALO_A + rows, :]
    for j in range(K_SHORT - 1):
        shift = K_SHORT - 1 - j
        conv_a += caw_ref[j:j + 1, :] * ca_ref[HALO_A - shift:HALO_A - shift + rows, :]
    y_ref[:, 0:D_CONV] = (z_ref[0] * conv_a).astype(bf16)

    for c0 in range(0, D_MODEL, MXU_COLS):
        o_ref[:, c0:c0 + MXU_COLS] = h_ref[:, c0:c0 + MXU_COLS] + jnp.dot(
            y_ref[...], wout_ref[:, c0:c0 + MXU_COLS], preferred_element_type=f32)

    ca_ref[0:HALO_A, :] = ca_ref[rows:rows + HALO_A, :]
    hb_ref[0:HALO_B, :] = hb_ref[rows:rows + HALO_B, :]


def _mixer(h, norm_g, w_in, conv_a_w, conv_b_w, conv_b_bias, ln_g, ln_b, w_out):
    row = lambda i: (i, 0)
    return pl.pallas_call(
        _mixer_kernel,
        grid=(SEQ // MIX_ROWS,),
        in_specs=[
            pl.BlockSpec((MIX_ROWS, D_MODEL), row),
            _resident((1, D_MODEL)),
            _resident(w_in.shape),
            _resident(conv_a_w.shape),
            _resident(conv_b_w.shape),
            _resident((1, D_CONV)),
            _resident((1, D_CONV)),
            _resident((1, D_CONV)),
            _resident(w_out.shape),
        ],
        out_specs=pl.BlockSpec((MIX_ROWS, D_MODEL), row),
        out_shape=jax.ShapeDtypeStruct((SEQ, D_MODEL), f32),
        scratch_shapes=[
            pltpu.VMEM((MIX_ROWS, D_MODEL), bf16),
            pltpu.VMEM((3, MIX_ROWS, D_CONV), f32),
            pltpu.VMEM((HALO_A + MIX_ROWS, D_CONV), f32),
            pltpu.VMEM((HALO_B + MIX_ROWS, D_CONV), f32),
            pltpu.VMEM((SUBLANES - 1, HALO_B + MIX_ROWS, D_CONV), f32),
            pltpu.VMEM((MIX_ROWS, D_CONV), f32),
            pltpu.VMEM((MIX_ROWS, 2 * D_CONV), bf16),
        ],
        compiler_params=pltpu.CompilerParams(
            dimension_semantics=("arbitrary",),
            vmem_limit_bytes=VMEM_LIMIT_BYTES),
        name="conv_mixer",
    )(h, norm_g, w_in, conv_a_w, conv_b_w, conv_b_bias, ln_g, ln_b, w_out)


def _mem_fold_kernel(mem_ref, g_ref, wq_ref, wk_ref, wv_ref, wo_ref, wqk_ref, vo_ref):
    mem_n = _rmsnorm(mem_ref[...], g_ref[...]).astype(bf16)
    k = jnp.dot(mem_n, wk_ref[...].astype(bf16), preferred_element_type=f32).astype(bf16)
    v = jnp.dot(mem_n, wv_ref[...].astype(bf16), preferred_element_type=f32).astype(bf16)
    wqk_ref[...] = lax.dot_general(wq_ref[...].astype(bf16), k, (((1,), (1,)), ((), ())),
                                   preferred_element_type=f32).astype(bf16)
    vo_ref[...] = jnp.dot(v, wo_ref[...].astype(bf16), preferred_element_type=f32).astype(bf16)


def _mem_fold(mem, norm_g, w_q, w_kv, w_o):
    hd = MEM_HEAD_DIM
    return pl.pallas_call(
        _mem_fold_kernel,
        grid=(N_MEM_HEADS,),
        in_specs=[
            pl.BlockSpec((MEM_LEN, D_MODEL), lambda h: (0, 0)),
            pl.BlockSpec((1, D_MODEL), lambda h: (0, 0)),
            pl.BlockSpec((D_MODEL, hd), lambda h: (0, h)),
            pl.BlockSpec((D_MODEL, hd), lambda h: (0, h)),
            pl.BlockSpec((D_MODEL, hd), lambda h: (0, h + N_MEM_HEADS)),
            pl.BlockSpec((hd, D_MODEL), lambda h: (h, 0)),
        ],
        out_specs=(pl.BlockSpec((D_MODEL, MEM_LEN), lambda h: (0, h)),
                   pl.BlockSpec((MEM_LEN, D_MODEL), lambda h: (h, 0))),
        out_shape=(jax.ShapeDtypeStruct((D_MODEL, N_MEM_HEADS * MEM_LEN), bf16),
                   jax.ShapeDtypeStruct((N_MEM_HEADS * MEM_LEN, D_MODEL), bf16)),
        compiler_params=pltpu.CompilerParams(
            dimension_semantics=("arbitrary",),
            vmem_limit_bytes=VMEM_LIMIT_BYTES),
        name="mem_fold",
    )(mem, norm_g, w_q, w_kv, w_kv, w_o)


def _attn_kernel(h_ref, g_ref, wqk_ref, vo_ref, o_ref, u_ref, p_ref):
    x = h_ref[...]
    u_ref[...] = _rmsnorm(x, g_ref[...]).astype(bf16)
    for hd in range(N_MEM_HEADS):
        cols = slice(hd * MEM_LEN, (hd + 1) * MEM_LEN)
        s = jnp.dot(u_ref[...], wqk_ref[:, cols], preferred_element_type=f32) * (MEM_HEAD_DIM ** -0.5)
        e = jnp.exp(s - jnp.max(s, axis=-1, keepdims=True))
        p_ref[:, cols] = (e / jnp.sum(e, axis=-1, keepdims=True)).astype(bf16)
    o_ref[...] = x + jnp.dot(p_ref[...], vo_ref[...], preferred_element_type=f32)


def _attn(h, norm_g, w_qk, vo):
    row = lambda i: (i, 0)
    return pl.pallas_call(
        _attn_kernel,
        grid=(SEQ // ATT_ROWS,),
        in_specs=[
            pl.BlockSpec((ATT_ROWS, D_MODEL), row),
            _resident((1, D_MODEL)),
            _resident(w_qk.shape),
            _resident(vo.shape),
        ],
        out_specs=pl.BlockSpec((ATT_ROWS, D_MODEL), row),
        out_shape=jax.ShapeDtypeStruct((SEQ, D_MODEL), f32),
        scratch_shapes=[pltpu.VMEM((ATT_ROWS, D_MODEL), bf16),
                        pltpu.VMEM((ATT_ROWS, N_MEM_HEADS * MEM_LEN), bf16)],
        compiler_params=pltpu.CompilerParams(
            dimension_semantics=("arbitrary",),
            vmem_limit_bytes=VMEM_LIMIT_BYTES),
        name="mem_attn",
    )(h, norm_g, w_qk, vo)


def kernel(x, mem, ffn1_norm, ffn1_w_gate_up, ffn1_w_down, mix_norm, w_mix_in, conv_a_w, conv_b_w,
           conv_b_bias, ln_b_gain, ln_b_bias, w_mix_out, mem_q_norm, mem_kv_norm, w_mem_q, w_mem_kv,
           w_mem_o, ffn2_norm, ffn2_w_gate_up, ffn2_w_down, final_norm):
    assert x.shape == (1, SEQ, D_MODEL) and mem.shape == (1, MEM_LEN, D_MODEL)
    assert ffn1_norm.shape[0] == 1, "one layer"
    h = x[0]
    final_g = final_norm[None, :]

    h = _ffn(h, ffn1_norm, ffn1_w_gate_up[0].astype(bf16), ffn1_w_down[0].astype(bf16), final_g,
             final_norm=False)
    h = _mixer(h, mix_norm, w_mix_in[0].astype(bf16), conv_a_w[0], conv_b_w[0], conv_b_bias,
               ln_b_gain, ln_b_bias, w_mix_out[0].astype(bf16))
    w_qk, vo = _mem_fold(mem[0], mem_kv_norm, w_mem_q[0], w_mem_kv[0], w_mem_o[0])
    h = _attn(h, mem_q_norm, w_qk, vo)
    h = _ffn(h, ffn2_norm, ffn2_w_gate_up[0].astype(bf16), ffn2_w_down[0].astype(bf16), final_g,
             final_norm=True)
    return h[None]
```

```python
import functools

import jax
import jax.numpy as jnp
from jax import lax
from jax.experimental import pallas as pl
from jax.experimental.pallas import tpu as pltpu

D_MODEL = 2048
SEQ = 16384
MEM_LEN = 256
D_FF = 5632
FFN_RES = 0.5
D_CONV = 1024
K_SHORT = 3
K_CONF = 31
N_MEM_HEADS = 4
MEM_HEAD_DIM = 512
RMS_EPS = 1e-6
LN_EPS = 1e-5

SUBLANES = 8
LANES = 128
MXU_COLS = 512
VMEM_LIMIT_BYTES = 60 * 1024 * 1024

FFN_ROWS = 1024
FFN_COLS = 512
MIX_ROWS = 256
ATT_ROWS = 1024
HALO_A = SUBLANES
HALO_B = 32
CONV_ROW_CHUNK = 64

f32 = jnp.float32
bf16 = jnp.bfloat16


def _rmsnorm(x, g):
    return x * lax.rsqrt(jnp.mean(x * x, axis=-1, keepdims=True) + RMS_EPS) * g


def _resident(shape):
    return pl.BlockSpec(shape, lambda *_: (0,) * len(shape), pipeline_mode=pl.Buffered(1))


def _ffn_kernel(x_ref, g_ref, wg_ref, wu_ref, wd_ref, fin_ref, *rest, final_norm, n_riders):
    cast_in, (o_ref, *cast_out), (u_ref,) = rest[:n_riders], rest[n_riders:-1], rest[-1:]
    j = pl.program_id(1)

    @pl.when(j == 0)
    def _():
        x = x_ref[...]
        u_ref[...] = _rmsnorm(x, g_ref[...]).astype(bf16)
        o_ref[...] = x

    u = u_ref[...]
    gate = jnp.dot(u, wg_ref[...], preferred_element_type=f32)
    up = jnp.dot(u, wu_ref[...], preferred_element_type=f32)
    act = (gate * jax.nn.sigmoid(gate) * (FFN_RES * up)).astype(bf16)
    o_ref[...] += jnp.dot(act, wd_ref[...], preferred_element_type=f32)

    for src, dst in zip(cast_in, cast_out):
        if len(dst.shape) == 3:
            for b in range(dst.shape[0]):
                dst[b] = src[:, b * FFN_COLS:(b + 1) * FFN_COLS].astype(bf16)
        else:
            dst[...] = src[...].astype(bf16)

    if final_norm:
        @pl.when(j == pl.num_programs(1) - 1)
        def _():
            o_ref[...] = _rmsnorm(o_ref[...], fin_ref[...])


def _gate_up_specs(w_gate_up):
    n_col = D_FF // FFN_COLS
    if w_gate_up.ndim == 3:
        return (pl.BlockSpec((None, D_MODEL, FFN_COLS), lambda i, j: (j, 0, 0)),
                pl.BlockSpec((None, D_MODEL, FFN_COLS), lambda i, j: (j + n_col, 0, 0)))
    return (pl.BlockSpec((D_MODEL, FFN_COLS), lambda i, j: (0, j)),
            pl.BlockSpec((D_MODEL, FFN_COLS), lambda i, j: (0, j + n_col)))


def _cast_riders(ffn_w_gate_up, ffn_w_down, w_mix_in, w_mix_out):
    n_row, n_col = SEQ // FFN_ROWS, D_FF // FFN_COLS
    gu_rows, gu_cols = D_MODEL // n_row, 2 * D_FF // n_col
    dn_rows = D_FF // (n_row * n_col)
    mi_cols, mo_cols = MXU_COLS, MXU_COLS // 2
    mi_steps, mo_steps = w_mix_in.shape[1] // mi_cols, w_mix_out.shape[1] // mo_cols
    assert gu_cols == 2 * FFN_COLS and dn_rows % 16 == 0 and mi_steps <= n_col and mo_steps <= n_col
    srcs = [ffn_w_gate_up, ffn_w_down, w_mix_in, w_mix_out]
    in_specs = [
        pl.BlockSpec((gu_rows, gu_cols), lambda i, j: (i, j)),
        pl.BlockSpec((dn_rows, D_MODEL), lambda i, j: (i * n_col + j, 0)),
        pl.BlockSpec((gu_rows, mi_cols), lambda i, j: (i, jnp.minimum(j, mi_steps - 1))),
        pl.BlockSpec((gu_rows, mo_cols), lambda i, j: (i, jnp.minimum(j, mo_steps - 1))),
    ]
    out_shapes = [
        jax.ShapeDtypeStruct((2 * n_col, D_MODEL, FFN_COLS), bf16),
        jax.ShapeDtypeStruct(ffn_w_down.shape, bf16),
        jax.ShapeDtypeStruct(w_mix_in.shape, bf16),
        jax.ShapeDtypeStruct(w_mix_out.shape, bf16),
    ]
    out_specs = [
        pl.BlockSpec((2, gu_rows, FFN_COLS), lambda i, j: (j, i, 0)),
        in_specs[1], in_specs[2], in_specs[3],
    ]
    return srcs, in_specs, out_shapes, out_specs


def _ffn(h, norm_g, w_gate_up, w_down, final_g, *, final_norm, riders=None):
    srcs, rider_in, rider_shapes, rider_out = riders if riders else ([], [], [], [])
    gate_spec, up_spec = _gate_up_specs(w_gate_up)
    outs = pl.pallas_call(
        functools.partial(_ffn_kernel, final_norm=final_norm, n_riders=len(srcs)),
        grid=(SEQ // FFN_ROWS, D_FF // FFN_COLS),
        in_specs=[
            pl.BlockSpec((FFN_ROWS, D_MODEL), lambda i, j: (i, 0)),
            pl.BlockSpec((1, D_MODEL), lambda i, j: (0, 0)),
            gate_spec,
            up_spec,
            pl.BlockSpec((FFN_COLS, D_MODEL), lambda i, j: (j, 0)),
            pl.BlockSpec((1, D_MODEL), lambda i, j: (0, 0)),
            *rider_in,
        ],
        out_specs=[pl.BlockSpec((FFN_ROWS, D_MODEL), lambda i, j: (i, 0)), *rider_out],
        out_shape=[jax.ShapeDtypeStruct((SEQ, D_MODEL), f32), *rider_shapes],
        scratch_shapes=[pltpu.VMEM((FFN_ROWS, D_MODEL), bf16)],
        compiler_params=pltpu.CompilerParams(
            dimension_semantics=("arbitrary", "arbitrary"),
            vmem_limit_bytes=VMEM_LIMIT_BYTES),
        name="ffn_final" if final_norm else "ffn",
    )(h, norm_g, w_gate_up, w_gate_up, w_down, final_g, *srcs)
    return outs[0], outs[1:]


def _mixer_kernel(h_ref, g_ref, win_ref, caw_ref, cbw_ref, cbb_ref, lng_ref, lnb_ref, wout_ref,
                  o_ref, u_ref, z_ref, ca_ref, hb_ref, sh_ref, cv_ref, y_ref):
    rows = h_ref.shape[0]

    @pl.when(pl.program_id(0) == 0)
    def _():
        ca_ref[0:HALO_A, :] = jnp.zeros((HALO_A, D_CONV), f32)
        hb_ref[0:HALO_B, :] = jnp.zeros((HALO_B, D_CONV), f32)

    x = h_ref[...]
    u_ref[...] = _rmsnorm(x, g_ref[...]).astype(bf16)

    def proj(col0):
        return jnp.dot(u_ref[...], win_ref[:, col0:col0 + MXU_COLS], preferred_element_type=f32)

    for c0 in range(0, D_CONV, MXU_COLS):
        hb_ref[HALO_B:HALO_B + rows, c0:c0 + MXU_COLS] = (
            proj(3 * D_CONV + c0) * jax.nn.sigmoid(proj(4 * D_CONV + c0)))
    for k in range(3):
        for c0 in range(0, D_CONV, MXU_COLS):
            z_ref[k, :, c0:c0 + MXU_COLS] = proj(k * D_CONV + c0)
    for r in range(1, SUBLANES):
        sh_ref[r - 1, SUBLANES:HALO_B + rows, :] = hb_ref[SUBLANES - r:HALO_B + rows - r, :]
    for c in range(D_CONV // LANES):
        cols = slice(c * LANES, (c + 1) * LANES)
        taps = [cbw_ref[j:j + 1, cols] for j in range(K_CONF)]
        bias = cbb_ref[:, cols]
        for rc in range(rows // CONV_ROW_CHUNK):
            r0 = HALO_B + rc * CONV_ROW_CHUNK
            acc = jnp.broadcast_to(bias, (CONV_ROW_CHUNK, LANES))
            for j in range(K_CONF):
                q, r = divmod(K_CONF - 1 - j, SUBLANES)
                start = r0 - q * SUBLANES
                if r == 0:
                    src = hb_ref[start:start + CONV_ROW_CHUNK, cols]
                else:
                    src = sh_ref[r - 1, start:start + CONV_ROW_CHUNK, cols]
                acc = acc + taps[j] * src
            cv_ref[rc * CONV_ROW_CHUNK:(rc + 1) * CONV_ROW_CHUNK, cols] = acc
    cv = cv_ref[...]
    mu = jnp.mean(cv, axis=-1, keepdims=True)
    cen = cv - mu
    var = jnp.mean(cen * cen, axis=-1, keepdims=True)
    ln = cen * lax.rsqrt(var + LN_EPS) * lng_ref[...] + lnb_ref[...]
    y_ref[:, D_CONV:2 * D_CONV] = (ln * jax.nn.sigmoid(ln)).astype(bf16)

    ca_ref[HALO_A:HALO_A + rows, :] = z_ref[1] * z_ref[2]
    conv_a = caw_ref[K_SHORT - 1:K_SHORT, :] * ca_ref[HALO_A:HALO_A + rows, :]
    for j in range(K_SHORT - 1):
        shift = K_SHORT - 1 - j
        conv_a += caw_ref[j:j + 1, :] * ca_ref[HALO_A - shift:HALO_A - shift + rows, :]
    y_ref[:, 0:D_CONV] = (z_ref[0] * conv_a).astype(bf16)

    for c0 in range(0, D_MODEL, MXU_COLS):
        o_ref[:, c0:c0 + MXU_COLS] = h_ref[:, c0:c0 + MXU_COLS] + jnp.dot(
            y_ref[...], wout_ref[:, c0:c0 + MXU_COLS], preferred_element_type=f32)

    ca_ref[0:HALO_A, :] = ca_ref[rows:rows + HALO_A, :]
    hb_ref[0:HALO_B, :] = hb_ref[rows:rows + HALO_B, :]


def _mixer(h, norm_g, w_in, conv_a_w, conv_b_w, conv_b_bias, ln_g, ln_b, w_out):
    row = lambda i: (i, 0)
    return pl.pallas_call(
        _mixer_kernel,
        grid=(SEQ // MIX_ROWS,),
        in_specs=[
            pl.BlockSpec((MIX_ROWS, D_MODEL), row),
            _resident((1, D_MODEL)),
            _resident(w_in.shape),
            _resident(conv_a_w.shape),
            _resident(conv_b_w.shape),
            _resident((1, D_CONV)),
            _resident((1, D_CONV)),
            _resident((1, D_CONV)),
            _resident(w_out.shape),
        ],
        out_specs=pl.BlockSpec((MIX_ROWS, D_MODEL), row),
        out_shape=jax.ShapeDtypeStruct((SEQ, D_MODEL), f32),
        scratch_shapes=[
            pltpu.VMEM((MIX_ROWS, D_MODEL), bf16),
            pltpu.VMEM((3, MIX_ROWS, D_CONV), f32),
            pltpu.VMEM((HALO_A + MIX_ROWS, D_CONV), f32),
            pltpu.VMEM((HALO_B + MIX_ROWS, D_CONV), f32),
            pltpu.VMEM((SUBLANES - 1, HALO_B + MIX_ROWS, D_CONV), f32),
            pltpu.VMEM((MIX_ROWS, D_CONV), f32),
            pltpu.VMEM((MIX_ROWS, 2 * D_CONV), bf16),
        ],
        compiler_params=pltpu.CompilerParams(
            dimension_semantics=("arbitrary",),
            vmem_limit_bytes=VMEM_LIMIT_BYTES),
        name="conv_mixer",
    )(h, norm_g, w_in, conv_a_w, conv_b_w, conv_b_bias, ln_g, ln_b, w_out)


def _mem_fold_kernel(mem_ref, g_ref, wq_ref, wk_ref, wv_ref, wo_ref, wqk_ref, vo_ref):
    mem_n = _rmsnorm(mem_ref[...], g_ref[...]).astype(bf16)
    k = jnp.dot(mem_n, wk_ref[...].astype(bf16), preferred_element_type=f32).astype(bf16)
    v = jnp.dot(mem_n, wv_ref[...].astype(bf16), preferred_element_type=f32).astype(bf16)
    wqk_ref[...] = lax.dot_general(wq_ref[...].astype(bf16), k, (((1,), (1,)), ((), ())),
                                   preferred_element_type=f32).astype(bf16)
    vo_ref[...] = jnp.dot(v, wo_ref[...].astype(bf16), preferred_element_type=f32).astype(bf16)


def _mem_fold(mem, norm_g, w_q, w_kv, w_o):
    hd = MEM_HEAD_DIM
    return pl.pallas_call(
        _mem_fold_kernel,
        grid=(N_MEM_HEADS,),
        in_specs=[
            pl.BlockSpec((MEM_LEN, D_MODEL), lambda h: (0, 0)),
            pl.BlockSpec((1, D_MODEL), lambda h: (0, 0)),
            pl.BlockSpec((D_MODEL, hd), lambda h: (0, h)),
            pl.BlockSpec((D_MODEL, hd), lambda h: (0, h)),
            pl.BlockSpec((D_MODEL, hd), lambda h: (0, h + N_MEM_HEADS)),
            pl.BlockSpec((hd, D_MODEL), lambda h: (h, 0)),
        ],
        out_specs=(pl.BlockSpec((D_MODEL, MEM_LEN), lambda h: (0, h)),
                   pl.BlockSpec((MEM_LEN, D_MODEL), lambda h: (h, 0))),
        out_shape=(jax.ShapeDtypeStruct((D_MODEL, N_MEM_HEADS * MEM_LEN), bf16),
                   jax.ShapeDtypeStruct((N_MEM_HEADS * MEM_LEN, D_MODEL), bf16)),
        compiler_params=pltpu.CompilerParams(
            dimension_semantics=("arbitrary",),
            vmem_limit_bytes=VMEM_LIMIT_BYTES),
        name="mem_fold",
    )(mem, norm_g, w_q, w_kv, w_kv, w_o)


def _attn_kernel(h_ref, g_ref, wqk_ref, vo_ref, o_ref, u_ref, p_ref):
    x = h_ref[...]
    u_ref[...] = _rmsnorm(x, g_ref[...]).astype(bf16)
    for hd in range(N_MEM_HEADS):
        cols = slice(hd * MEM_LEN, (hd + 1) * MEM_LEN)
        s = jnp.dot(u_ref[...], wqk_ref[:, cols], preferred_element_type=f32) * (MEM_HEAD_DIM ** -0.5)
        e = jnp.exp(s - jnp.max(s, axis=-1, keepdims=True))
        p_ref[:, cols] = (e / jnp.sum(e, axis=-1, keepdims=True)).astype(bf16)
    o_ref[...] = x + jnp.dot(p_ref[...], vo_ref[...], preferred_element_type=f32)


def _attn(h, norm_g, w_qk, vo):
    row = lambda i: (i, 0)
    return pl.pallas_call(
        _attn_kernel,
        grid=(SEQ // ATT_ROWS,),
        in_specs=[
            pl.BlockSpec((ATT_ROWS, D_MODEL), row),
            _resident((1, D_MODEL)),
            _resident(w_qk.shape),
            _resident(vo.shape),
        ],
        out_specs=pl.BlockSpec((ATT_ROWS, D_MODEL), row),
        out_shape=jax.ShapeDtypeStruct((SEQ, D_MODEL), f32),
        scratch_shapes=[pltpu.VMEM((ATT_ROWS, D_MODEL), bf16),
                        pltpu.VMEM((ATT_ROWS, N_MEM_HEADS * MEM_LEN), bf16)],
        compiler_params=pltpu.CompilerParams(
            dimension_semantics=("arbitrary",),
            vmem_limit_bytes=VMEM_LIMIT_BYTES),
        name="mem_attn",
    )(h, norm_g, w_qk, vo)


def kernel(x, mem, ffn1_norm, ffn1_w_gate_up, ffn1_w_down, mix_norm, w_mix_in, conv_a_w, conv_b_w,
           conv_b_bias, ln_b_gain, ln_b_bias, w_mix_out, mem_q_norm, mem_kv_norm, w_mem_q, w_mem_kv,
           w_mem_o, ffn2_norm, ffn2_w_gate_up, ffn2_w_down, final_norm):
    assert x.shape == (1, SEQ, D_MODEL) and mem.shape == (1, MEM_LEN, D_MODEL)
    assert ffn1_norm.shape[0] == 1, "one layer"
    h = x[0]
    final_g = final_norm[None, :]

    riders = _cast_riders(ffn2_w_gate_up[0], ffn2_w_down[0], w_mix_in[0], w_mix_out[0])
    h, (w2_gate_up, w2_down, w_in, w_out) = _ffn(
        h, ffn1_norm, ffn1_w_gate_up[0].astype(bf16), ffn1_w_down[0].astype(bf16), final_g,
        final_norm=False, riders=riders)
    h = _mixer(h, mix_norm, w_in, conv_a_w[0], conv_b_w[0], conv_b_bias, ln_b_gain, ln_b_bias, w_out)
    w_qk, vo = _mem_fold(mem[0], mem_kv_norm, w_mem_q[0], w_mem_kv[0], w_mem_o[0])
    h = _attn(h, mem_q_norm, w_qk, vo)
    h, _ = _ffn(h, ffn2_norm, w2_gate_up, w2_down, final_g, final_norm=True)
    return h[None]
```

```python
import functools

import jax
import jax.numpy as jnp
from jax import lax
from jax.experimental import pallas as pl
from jax.experimental.pallas import tpu as pltpu

D_MODEL = 2048
SEQ = 16384
MEM_LEN = 256
D_FF = 5632
FFN_RES = 0.5
D_CONV = 1024
K_SHORT = 3
K_CONF = 31
N_MEM_HEADS = 4
MEM_HEAD_DIM = 512
RMS_EPS = 1e-6
LN_EPS = 1e-5

SUBLANES = 8
LANES = 128
MXU_COLS = 512
VMEM_LIMIT_BYTES = 60 * 1024 * 1024

FFN_ROWS = 1024
FFN_COLS = 512
MIX_ROWS = 256
ATT_ROWS = 1024
HALO_A = SUBLANES
HALO_B = 32
CONV_ROW_CHUNKS = (32,) * 8
assert sum(CONV_ROW_CHUNKS) == MIX_ROWS and all(c % SUBLANES == 0 for c in CONV_ROW_CHUNKS)

f32 = jnp.float32
bf16 = jnp.bfloat16


def _rmsnorm(x, g):
    return x * lax.rsqrt(jnp.mean(x * x, axis=-1, keepdims=True) + RMS_EPS) * g


def _resident(shape):
    return pl.BlockSpec(shape, lambda *_: (0,) * len(shape), pipeline_mode=pl.Buffered(1))


def _ffn_kernel(x_ref, g_ref, wg_ref, wu_ref, wd_ref, fin_ref, *rest, final_norm, n_riders):
    cast_in, (o_ref, *cast_out), (u_ref,) = rest[:n_riders], rest[n_riders:-1], rest[-1:]
    j = pl.program_id(1)

    @pl.when(j == 0)
    def _():
        x = x_ref[...]
        u_ref[...] = _rmsnorm(x, g_ref[...]).astype(bf16)
        o_ref[...] = x

    u = u_ref[...]
    gate = jnp.dot(u, wg_ref[...], preferred_element_type=f32)
    up = jnp.dot(u, wu_ref[...], preferred_element_type=f32)
    act = (gate * jax.nn.sigmoid(gate) * (FFN_RES * up)).astype(bf16)
    o_ref[...] += jnp.dot(act, wd_ref[...], preferred_element_type=f32)

    for src, dst in zip(cast_in, cast_out):
        if len(dst.shape) == 3:
            for b in range(dst.shape[0]):
                dst[b] = src[:, b * FFN_COLS:(b + 1) * FFN_COLS].astype(bf16)
        else:
            dst[...] = src[...].astype(bf16)

    if final_norm:
        @pl.when(j == pl.num_programs(1) - 1)
        def _():
            o_ref[...] = _rmsnorm(o_ref[...], fin_ref[...])


def _gate_up_specs(w_gate_up):
    n_col = D_FF // FFN_COLS
    if w_gate_up.ndim == 3:
        return (pl.BlockSpec((None, D_MODEL, FFN_COLS), lambda i, j: (j, 0, 0)),
                pl.BlockSpec((None, D_MODEL, FFN_COLS), lambda i, j: (j + n_col, 0, 0)))
    return (pl.BlockSpec((D_MODEL, FFN_COLS), lambda i, j: (0, j)),
            pl.BlockSpec((D_MODEL, FFN_COLS), lambda i, j: (0, j + n_col)))


def _cast_riders(ffn_w_gate_up, ffn_w_down, w_mix_in, w_mix_out):
    n_row, n_col = SEQ // FFN_ROWS, D_FF // FFN_COLS
    gu_rows, gu_cols = D_MODEL // n_row, 2 * D_FF // n_col
    dn_rows = D_FF // (n_row * n_col)
    mi_cols, mo_cols = MXU_COLS, MXU_COLS // 2
    mi_steps, mo_steps = w_mix_in.shape[1] // mi_cols, w_mix_out.shape[1] // mo_cols
    assert gu_cols == 2 * FFN_COLS and dn_rows % 16 == 0 and mi_steps <= n_col and mo_steps <= n_col
    srcs = [ffn_w_gate_up, ffn_w_down, w_mix_in, w_mix_out]
    in_specs = [
        pl.BlockSpec((gu_rows, gu_cols), lambda i, j: (i, j)),
        pl.BlockSpec((dn_rows, D_MODEL), lambda i, j: (i * n_col + j, 0)),
        pl.BlockSpec((gu_rows, mi_cols), lambda i, j: (i, jnp.minimum(j, mi_steps - 1))),
        pl.BlockSpec((gu_rows, mo_cols), lambda i, j: (i, jnp.minimum(j, mo_steps - 1))),
    ]
    out_shapes = [
        jax.ShapeDtypeStruct((2 * n_col, D_MODEL, FFN_COLS), bf16),
        jax.ShapeDtypeStruct(ffn_w_down.shape, bf16),
        jax.ShapeDtypeStruct(w_mix_in.shape, bf16),
        jax.ShapeDtypeStruct(w_mix_out.shape, bf16),
    ]
    out_specs = [
        pl.BlockSpec((2, gu_rows, FFN_COLS), lambda i, j: (j, i, 0)),
        in_specs[1], in_specs[2], in_specs[3],
    ]
    return srcs, in_specs, out_shapes, out_specs


def _ffn(h, norm_g, w_gate_up, w_down, final_g, *, final_norm, riders=None):
    srcs, rider_in, rider_shapes, rider_out = riders if riders else ([], [], [], [])
    gate_spec, up_spec = _gate_up_specs(w_gate_up)
    outs = pl.pallas_call(
        functools.partial(_ffn_kernel, final_norm=final_norm, n_riders=len(srcs)),
        grid=(SEQ // FFN_ROWS, D_FF // FFN_COLS),
        in_specs=[
            pl.BlockSpec((FFN_ROWS, D_MODEL), lambda i, j: (i, 0)),
            pl.BlockSpec((1, D_MODEL), lambda i, j: (0, 0)),
            gate_spec,
            up_spec,
            pl.BlockSpec((FFN_COLS, D_MODEL), lambda i, j: (j, 0)),
            pl.BlockSpec((1, D_MODEL), lambda i, j: (0, 0)),
            *rider_in,
        ],
        out_specs=[pl.BlockSpec((FFN_ROWS, D_MODEL), lambda i, j: (i, 0)), *rider_out],
        out_shape=[jax.ShapeDtypeStruct((SEQ, D_MODEL), f32), *rider_shapes],
        scratch_shapes=[pltpu.VMEM((FFN_ROWS, D_MODEL), bf16)],
        compiler_params=pltpu.CompilerParams(
            dimension_semantics=("arbitrary", "arbitrary"),
            vmem_limit_bytes=VMEM_LIMIT_BYTES),
        name="ffn_final" if final_norm else "ffn",
    )(h, norm_g, w_gate_up, w_gate_up, w_down, final_g, *srcs)
    return outs[0], outs[1:]


def _mixer_kernel(zero_ref, h_ref, g_ref, win_ref, caw_ref, cbw_ref, cbb_ref, lng_ref, lnb_ref,
                  wout_ref, o_ref, u_ref, z_ref, ca_ref, cb_ref, y_ref):
    rows = h_ref.shape[0]
    span = HALO_B + rows
    cv0 = SUBLANES * span

    @pl.when(pl.program_id(0) == 0)
    def _():
        ca_ref[0:HALO_A, :] = jnp.zeros((HALO_A, D_CONV), f32)
        cb_ref[0:HALO_B, :] = jnp.zeros((HALO_B, D_CONV), f32)

    x = h_ref[...]
    u_ref[...] = _rmsnorm(x, g_ref[...]).astype(bf16)

    def proj(col0):
        return jnp.dot(u_ref[...], win_ref[:, col0:col0 + MXU_COLS], preferred_element_type=f32)

    for c0 in range(0, D_CONV, MXU_COLS):
        cb_ref[HALO_B:span, c0:c0 + MXU_COLS] = (
            proj(3 * D_CONV + c0) * jax.nn.sigmoid(proj(4 * D_CONV + c0)))
    for k in range(3):
        for c0 in range(0, D_CONV, MXU_COLS):
            z_ref[k, :, c0:c0 + MXU_COLS] = proj(k * D_CONV + c0)

    for r in range(1, SUBLANES):
        cb_ref[r * span + SUBLANES:(r + 1) * span, :] = cb_ref[SUBLANES - r:span - r, :]
    hidden0 = pl.multiple_of(zero_ref[0], SUBLANES)
    for c in range(D_CONV // LANES):
        cols = slice(c * LANES, (c + 1) * LANES)
        taps = [cbw_ref[j:j + 1, cols] for j in range(K_CONF)]
        bias = cbb_ref[:, cols]
        row0 = 0
        for chunk in CONV_ROW_CHUNKS:
            acc = jnp.broadcast_to(bias, (chunk, LANES))
            for j in range(K_CONF):
                q, r = divmod(K_CONF - 1 - j, SUBLANES)
                start = r * span + HALO_B + row0 - q * SUBLANES
                acc = acc + taps[j] * cb_ref[start:start + chunk, cols]
            cb_ref[pl.ds(hidden0 + cv0 + row0, chunk), cols] = acc
            row0 += chunk
    cv = cb_ref[cv0:cv0 + rows, :]
    mu = jnp.mean(cv, axis=-1, keepdims=True)
    cen = cv - mu
    var = jnp.mean(cen * cen, axis=-1, keepdims=True)
    ln = cen * lax.rsqrt(var + LN_EPS) * lng_ref[...] + lnb_ref[...]
    y_ref[:, D_CONV:2 * D_CONV] = (ln * jax.nn.sigmoid(ln)).astype(bf16)

    ca_ref[HALO_A:HALO_A + rows, :] = z_ref[1] * z_ref[2]
    conv_a = caw_ref[K_SHORT - 1:K_SHORT, :] * ca_ref[HALO_A:HALO_A + rows, :]
    for j in range(K_SHORT - 1):
        shift = K_SHORT - 1 - j
        conv_a += caw_ref[j:j + 1, :] * ca_ref[HALO_A - shift:HALO_A - shift + rows, :]
    y_ref[:, 0:D_CONV] = (z_ref[0] * conv_a).astype(bf16)

    for c0 in range(0, D_MODEL, MXU_COLS):
        o_ref[:, c0:c0 + MXU_COLS] = h_ref[:, c0:c0 + MXU_COLS] + jnp.dot(
            y_ref[...], wout_ref[:, c0:c0 + MXU_COLS], preferred_element_type=f32)

    ca_ref[0:HALO_A, :] = ca_ref[rows:rows + HALO_A, :]
    cb_ref[0:HALO_B, :] = cb_ref[rows:span, :]


def _mixer(h, norm_g, w_in, conv_a_w, conv_b_w, conv_b_bias, ln_g, ln_b, w_out):
    row = lambda i: (i, 0)
    return pl.pallas_call(
        _mixer_kernel,
        grid=(SEQ // MIX_ROWS,),
        in_specs=[
            pl.BlockSpec(memory_space=pltpu.SMEM),
            pl.BlockSpec((MIX_ROWS, D_MODEL), row),
            _resident((1, D_MODEL)),
            _resident(w_in.shape),
            _resident(conv_a_w.shape),
            _resident(conv_b_w.shape),
            _resident((1, D_CONV)),
            _resident((1, D_CONV)),
            _resident((1, D_CONV)),
            _resident(w_out.shape),
        ],
        out_specs=pl.BlockSpec((MIX_ROWS, D_MODEL), row),
        out_shape=jax.ShapeDtypeStruct((SEQ, D_MODEL), f32),
        scratch_shapes=[
            pltpu.VMEM((MIX_ROWS, D_MODEL), bf16),
            pltpu.VMEM((3, MIX_ROWS, D_CONV), f32),
            pltpu.VMEM((HALO_A + MIX_ROWS, D_CONV), f32),
            pltpu.VMEM((SUBLANES * (HALO_B + MIX_ROWS) + MIX_ROWS, D_CONV), f32),
            pltpu.VMEM((MIX_ROWS, 2 * D_CONV), bf16),
        ],
        compiler_params=pltpu.CompilerParams(
            dimension_semantics=("arbitrary",),
            vmem_limit_bytes=VMEM_LIMIT_BYTES),
        name="conv_mixer",
    )(jnp.zeros((1,), jnp.int32), h, norm_g, w_in, conv_a_w, conv_b_w, conv_b_bias, ln_g, ln_b, w_out)


def _mem_fold_kernel(mem_ref, g_ref, wq_ref, wk_ref, wv_ref, wo_ref, wqk_ref, vo_ref):
    mem_n = _rmsnorm(mem_ref[...], g_ref[...]).astype(bf16)
    k = jnp.dot(mem_n, wk_ref[...].astype(bf16), preferred_element_type=f32).astype(bf16)
    v = jnp.dot(mem_n, wv_ref[...].astype(bf16), preferred_element_type=f32).astype(bf16)
    wqk_ref[...] = lax.dot_general(wq_ref[...].astype(bf16), k, (((1,), (1,)), ((), ())),
                                   preferred_element_type=f32).astype(bf16)
    vo_ref[...] = jnp.dot(v, wo_ref[...].astype(bf16), preferred_element_type=f32).astype(bf16)


def _mem_fold(mem, norm_g, w_q, w_kv, w_o):
    hd = MEM_HEAD_DIM
    return pl.pallas_call(
        _mem_fold_kernel,
        grid=(N_MEM_HEADS,),
        in_specs=[
            pl.BlockSpec((MEM_LEN, D_MODEL), lambda h: (0, 0)),
            pl.BlockSpec((1, D_MODEL), lambda h: (0, 0)),
            pl.BlockSpec((D_MODEL, hd), lambda h: (0, h)),
            pl.BlockSpec((D_MODEL, hd), lambda h: (0, h)),
            pl.BlockSpec((D_MODEL, hd), lambda h: (0, h + N_MEM_HEADS)),
            pl.BlockSpec((hd, D_MODEL), lambda h: (h, 0)),
        ],
        out_specs=(pl.BlockSpec((D_MODEL, MEM_LEN), lambda h: (0, h)),
                   pl.BlockSpec((MEM_LEN, D_MODEL), lambda h: (h, 0))),
        out_shape=(jax.ShapeDtypeStruct((D_MODEL, N_MEM_HEADS * MEM_LEN), bf16),
                   jax.ShapeDtypeStruct((N_MEM_HEADS * MEM_LEN, D_MODEL), bf16)),
        compiler_params=pltpu.CompilerParams(
            dimension_semantics=("arbitrary",),
            vmem_limit_bytes=VMEM_LIMIT_BYTES),
        name="mem_fold",
    )(mem, norm_g, w_q, w_kv, w_kv, w_o)


def _attn_kernel(h_ref, g_ref, wqk_ref, vo_ref, o_ref, u_ref, p_ref):
    x = h_ref[...]
    u_ref[...] = _rmsnorm(x, g_ref[...]).astype(bf16)
    for hd in range(N_MEM_HEADS):
        cols = slice(hd * MEM_LEN, (hd + 1) * MEM_LEN)
        s = jnp.dot(u_ref[...], wqk_ref[:, cols], preferred_element_type=f32) * (MEM_HEAD_DIM ** -0.5)
        e = jnp.exp(s - jnp.max(s, axis=-1, keepdims=True))
        p_ref[:, cols] = (e / jnp.sum(e, axis=-1, keepdims=True)).astype(bf16)
    o_ref[...] = x + jnp.dot(p_ref[...], vo_ref[...], preferred_element_type=f32)


def _attn(h, norm_g, w_qk, vo):
    row = lambda i: (i, 0)
    return pl.pallas_call(
        _attn_kernel,
        grid=(SEQ // ATT_ROWS,),
        in_specs=[
            pl.BlockSpec((ATT_ROWS, D_MODEL), row),
            _resident((1, D_MODEL)),
            _resident(w_qk.shape),
            _resident(vo.shape),
        ],
        out_specs=pl.BlockSpec((ATT_ROWS, D_MODEL), row),
        out_shape=jax.ShapeDtypeStruct((SEQ, D_MODEL), f32),
        scratch_shapes=[pltpu.VMEM((ATT_ROWS, D_MODEL), bf16),
                        pltpu.VMEM((ATT_ROWS, N_MEM_HEADS * MEM_LEN), bf16)],
        compiler_params=pltpu.CompilerParams(
            dimension_semantics=("arbitrary",),
            vmem_limit_bytes=VMEM_LIMIT_BYTES),
        name="mem_attn",
    )(h, norm_g, w_qk, vo)


def kernel(x, mem, ffn1_norm, ffn1_w_gate_up, ffn1_w_down, mix_norm, w_mix_in, conv_a_w, conv_b_w,
           conv_b_bias, ln_b_gain, ln_b_bias, w_mix_out, mem_q_norm, mem_kv_norm, w_mem_q, w_mem_kv,
           w_mem_o, ffn2_norm, ffn2_w_gate_up, ffn2_w_down, final_norm):
    assert x.shape == (1, SEQ, D_MODEL) and mem.shape == (1, MEM_LEN, D_MODEL)
    assert ffn1_norm.shape[0] == 1, "one layer"
    h = x[0]
    final_g = final_norm[None, :]

    riders = _cast_riders(ffn2_w_gate_up[0], ffn2_w_down[0], w_mix_in[0], w_mix_out[0])
    h, (w2_gate_up, w2_down, w_in, w_out) = _ffn(
        h, ffn1_norm, ffn1_w_gate_up[0].astype(bf16), ffn1_w_down[0].astype(bf16), final_g,
        final_norm=False, riders=riders)
    h = _mixer(h, mix_norm, w_in, conv_a_w[0], conv_b_w[0], conv_b_bias, ln_b_gain, ln_b_bias, w_out)
    w_qk, vo = _mem_fold(mem[0], mem_kv_norm, w_mem_q[0], w_mem_kv[0], w_mem_o[0])
    h = _attn(h, mem_q_norm, w_qk, vo)
    h, _ = _ffn(h, ffn2_norm, w2_gate_up, w2_down, final_g, final_norm=True)
    return h[None]
```

```python
import functools

import jax
import jax.numpy as jnp
from jax import lax
from jax.experimental import pallas as pl
from jax.experimental.pallas import tpu as pltpu

D_MODEL = 2048
SEQ = 16384
MEM_LEN = 256
D_FF = 5632
FFN_RES = 0.5
D_CONV = 1024
K_SHORT = 3
K_CONF = 31
N_MEM_HEADS = 4
MEM_HEAD_DIM = 512
RMS_EPS = 1e-6
LN_EPS = 1e-5

SUBLANES = 8
LANES = 128
MXU_COLS = 512
VMEM_LIMIT_BYTES = 60 * 1024 * 1024

FFN_ROWS = 1024
FFN_COLS = 512
MIX_ROWS = 256
ATT_ROWS = 1024
HALO_A = SUBLANES
HALO_B = 32
CONV_ROW_CHUNKS = (32,) * 8
assert sum(CONV_ROW_CHUNKS) == MIX_ROWS and all(c % SUBLANES == 0 for c in CONV_ROW_CHUNKS)

f32 = jnp.float32
bf16 = jnp.bfloat16


def _rmsnorm(x, g):
    return x * lax.rsqrt(jnp.mean(x * x, axis=-1, keepdims=True) + RMS_EPS) * g


def _resident(shape):
    return pl.BlockSpec(shape, lambda *_: (0,) * len(shape), pipeline_mode=pl.Buffered(1))


def _ffn_kernel(x_ref, g_ref, wg_ref, wu_ref, wd_ref, fin_ref, *rest, final_norm, n_riders):
    cast_in, (o_ref, *cast_out), (u_ref,) = rest[:n_riders], rest[n_riders:-1], rest[-1:]
    j = pl.program_id(1)

    @pl.when(j == 0)
    def _():
        x = x_ref[...]
        u_ref[...] = _rmsnorm(x, g_ref[...]).astype(bf16)
        o_ref[...] = x

    u = u_ref[...]
    gate = jnp.dot(u, wg_ref[...], preferred_element_type=f32)
    up = jnp.dot(u, wu_ref[...], preferred_element_type=f32)
    act = (gate * jax.nn.sigmoid(gate) * (FFN_RES * up)).astype(bf16)
    o_ref[...] += jnp.dot(act, wd_ref[...], preferred_element_type=f32)

    for src, dst in zip(cast_in, cast_out):
        if len(dst.shape) == 3:
            for b in range(dst.shape[0]):
                dst[b] = src[:, b * FFN_COLS:(b + 1) * FFN_COLS].astype(bf16)
        else:
            dst[...] = src[...].astype(bf16)

    if final_norm:
        @pl.when(j == pl.num_programs(1) - 1)
        def _():
            o_ref[...] = _rmsnorm(o_ref[...], fin_ref[...])


def _gate_up_specs(w_gate_up):
    n_col = D_FF // FFN_COLS
    if w_gate_up.ndim == 3:
        return (pl.BlockSpec((None, D_MODEL, FFN_COLS), lambda i, j: (j, 0, 0)),
                pl.BlockSpec((None, D_MODEL, FFN_COLS), lambda i, j: (j + n_col, 0, 0)))
    return (pl.BlockSpec((D_MODEL, FFN_COLS), lambda i, j: (0, j)),
            pl.BlockSpec((D_MODEL, FFN_COLS), lambda i, j: (0, j + n_col)))


def _cast_riders(ffn_w_gate_up, ffn_w_down, w_mix_in, w_mix_out):
    n_row, n_col = SEQ // FFN_ROWS, D_FF // FFN_COLS
    gu_rows, gu_cols = D_MODEL // n_row, 2 * D_FF // n_col
    dn_rows = D_FF // (n_row * n_col)
    mi_cols, mo_cols = MXU_COLS, MXU_COLS // 2
    mi_steps, mo_steps = w_mix_in.shape[1] // mi_cols, w_mix_out.shape[1] // mo_cols
    assert gu_cols == 2 * FFN_COLS and dn_rows % 16 == 0 and mi_steps <= n_col and mo_steps <= n_col
    srcs = [ffn_w_gate_up, ffn_w_down, w_mix_in, w_mix_out]
    in_specs = [
        pl.BlockSpec((gu_rows, gu_cols), lambda i, j: (i, j)),
        pl.BlockSpec((dn_rows, D_MODEL), lambda i, j: (i * n_col + j, 0)),
        pl.BlockSpec((gu_rows, mi_cols), lambda i, j: (i, jnp.minimum(j, mi_steps - 1))),
        pl.BlockSpec((gu_rows, mo_cols), lambda i, j: (i, jnp.minimum(j, mo_steps - 1))),
    ]
    out_shapes = [
        jax.ShapeDtypeStruct((2 * n_col, D_MODEL, FFN_COLS), bf16),
        jax.ShapeDtypeStruct(ffn_w_down.shape, bf16),
        jax.ShapeDtypeStruct(w_mix_in.shape, bf16),
        jax.ShapeDtypeStruct(w_mix_out.shape, bf16),
    ]
    out_specs = [
        pl.BlockSpec((2, gu_rows, FFN_COLS), lambda i, j: (j, i, 0)),
        in_specs[1], in_specs[2], in_specs[3],
    ]
    return srcs, in_specs, out_shapes, out_specs


def _ffn(h, norm_g, w_gate_up, w_down, final_g, *, final_norm, riders=None):
    srcs, rider_in, rider_shapes, rider_out = riders if riders else ([], [], [], [])
    gate_spec, up_spec = _gate_up_specs(w_gate_up)
    outs = pl.pallas_call(
        functools.partial(_ffn_kernel, final_norm=final_norm, n_riders=len(srcs)),
        grid=(SEQ // FFN_ROWS, D_FF // FFN_COLS),
        in_specs=[
            pl.BlockSpec((FFN_ROWS, D_MODEL), lambda i, j: (i, 0)),
            pl.BlockSpec((1, D_MODEL), lambda i, j: (0, 0)),
            gate_spec,
            up_spec,
            pl.BlockSpec((FFN_COLS, D_MODEL), lambda i, j: (j, 0)),
            pl.BlockSpec((1, D_MODEL), lambda i, j: (0, 0)),
            *rider_in,
        ],
        out_specs=[pl.BlockSpec((FFN_ROWS, D_MODEL), lambda i, j: (i, 0)), *rider_out],
        out_shape=[jax.ShapeDtypeStruct((SEQ, D_MODEL), f32), *rider_shapes],
        scratch_shapes=[pltpu.VMEM((FFN_ROWS, D_MODEL), bf16)],
        compiler_params=pltpu.CompilerParams(
            dimension_semantics=("arbitrary", "arbitrary"),
            vmem_limit_bytes=VMEM_LIMIT_BYTES),
        name="ffn_final" if final_norm else "ffn",
    )(h, norm_g, w_gate_up, w_gate_up, w_down, final_g, *srcs)
    return outs[0], outs[1:]


def _ffn_tile_kernel(x_ref, g_ref, fin_ref, wgu_hbm, wd_hbm, o_ref, u_ref, gbuf, ubuf, dbuf, sem,
                     *, final_norm):
    i = pl.program_id(0)
    n_col = D_FF // FFN_COLS

    def chunk_copies(j, slot):
        return (pltpu.make_async_copy(wgu_hbm.at[j], gbuf.at[slot], sem.at[0, slot]),
                pltpu.make_async_copy(wgu_hbm.at[j + n_col], ubuf.at[slot], sem.at[1, slot]),
                pltpu.make_async_copy(wd_hbm.at[pl.ds(j * FFN_COLS, FFN_COLS)], dbuf.at[slot],
                                      sem.at[2, slot]))

    first_slot = (i * n_col) % 2

    @pl.when(i == 0)
    def _():
        for c in chunk_copies(0, 0):
            c.start()

    x = x_ref[...]
    u_ref[...] = _rmsnorm(x, g_ref[...]).astype(bf16)
    o_ref[...] = x
    for j in range(n_col):
        slot = (first_slot + j) % 2
        for c in chunk_copies((j + 1) % n_col, 1 - slot):
            c.start()
        for c in chunk_copies(j, slot):
            c.wait()
        u = u_ref[...]
        gate = jnp.dot(u, gbuf[slot], preferred_element_type=f32)
        up = jnp.dot(u, ubuf[slot], preferred_element_type=f32)
        act = (gate * jax.nn.sigmoid(gate) * (FFN_RES * up)).astype(bf16)
        o_ref[...] += jnp.dot(act, dbuf[slot], preferred_element_type=f32)
    if final_norm:
        o_ref[...] = _rmsnorm(o_ref[...], fin_ref[...])

    @pl.when(i == pl.num_programs(0) - 1)
    def _():
        for c in chunk_copies(0, (first_slot + n_col) % 2):
            c.wait()


def _ffn_tiles(h, norm_g, w_gate_up_blocked, w_down, final_g, *, final_norm):
    row = lambda i: (i, 0)
    return pl.pallas_call(
        functools.partial(_ffn_tile_kernel, final_norm=final_norm),
        grid=(SEQ // FFN_ROWS,),
        in_specs=[
            pl.BlockSpec((FFN_ROWS, D_MODEL), row),
            _resident((1, D_MODEL)),
            _resident((1, D_MODEL)),
            pl.BlockSpec(memory_space=pl.ANY),
            pl.BlockSpec(memory_space=pl.ANY),
        ],
        out_specs=pl.BlockSpec((FFN_ROWS, D_MODEL), row),
        out_shape=jax.ShapeDtypeStruct((SEQ, D_MODEL), f32),
        scratch_shapes=[
            pltpu.VMEM((FFN_ROWS, D_MODEL), bf16),
            pltpu.VMEM((2, D_MODEL, FFN_COLS), bf16),
            pltpu.VMEM((2, D_MODEL, FFN_COLS), bf16),
            pltpu.VMEM((2, FFN_COLS, D_MODEL), bf16),
            pltpu.SemaphoreType.DMA((3, 2)),
        ],
        compiler_params=pltpu.CompilerParams(
            dimension_semantics=("arbitrary",),
            vmem_limit_bytes=VMEM_LIMIT_BYTES),
        name="ffn_final" if final_norm else "ffn_tiles",
    )(h, norm_g, final_g, w_gate_up_blocked, w_down)


def _mixer_kernel(zero_ref, h_ref, g_ref, win_ref, caw_ref, cbw_ref, cbb_ref, lng_ref, lnb_ref,
                  wout_ref, o_ref, u_ref, z_ref, ca_ref, cb_ref, y_ref):
    rows = h_ref.shape[0]
    span = HALO_B + rows
    cv0 = SUBLANES * span

    @pl.when(pl.program_id(0) == 0)
    def _():
        ca_ref[0:HALO_A, :] = jnp.zeros((HALO_A, D_CONV), f32)
        cb_ref[0:HALO_B, :] = jnp.zeros((HALO_B, D_CONV), f32)

    x = h_ref[...]
    u_ref[...] = _rmsnorm(x, g_ref[...]).astype(bf16)

    def proj(col0):
        return jnp.dot(u_ref[...], win_ref[:, col0:col0 + MXU_COLS], preferred_element_type=f32)

    for c0 in range(0, D_CONV, MXU_COLS):
        cb_ref[HALO_B:span, c0:c0 + MXU_COLS] = (
            proj(3 * D_CONV + c0) * jax.nn.sigmoid(proj(4 * D_CONV + c0)))
    for k in range(3):
        for c0 in range(0, D_CONV, MXU_COLS):
            z_ref[k, :, c0:c0 + MXU_COLS] = proj(k * D_CONV + c0)

    for r in range(1, SUBLANES):
        cb_ref[r * span + SUBLANES:(r + 1) * span, :] = cb_ref[SUBLANES - r:span - r, :]
    hidden0 = pl.multiple_of(zero_ref[0], SUBLANES)
    for c in range(D_CONV // LANES):
        cols = slice(c * LANES, (c + 1) * LANES)
        taps = [cbw_ref[j:j + 1, cols] for j in range(K_CONF)]
        bias = cbb_ref[:, cols]
        row0 = 0
        for chunk in CONV_ROW_CHUNKS:
            acc = jnp.broadcast_to(bias, (chunk, LANES))
            for j in range(K_CONF):
                q, r = divmod(K_CONF - 1 - j, SUBLANES)
                start = r * span + HALO_B + row0 - q * SUBLANES
                acc = acc + taps[j] * cb_ref[start:start + chunk, cols]
            cb_ref[pl.ds(hidden0 + cv0 + row0, chunk), cols] = acc
            row0 += chunk
    cv = cb_ref[cv0:cv0 + rows, :]
    mu = jnp.mean(cv, axis=-1, keepdims=True)
    cen = cv - mu
    var = jnp.mean(cen * cen, axis=-1, keepdims=True)
    ln = cen * lax.rsqrt(var + LN_EPS) * lng_ref[...] + lnb_ref[...]
    y_ref[:, D_CONV:2 * D_CONV] = (ln * jax.nn.sigmoid(ln)).astype(bf16)

    ca_ref[HALO_A:HALO_A + rows, :] = z_ref[1] * z_ref[2]
    conv_a = caw_ref[K_SHORT - 1:K_SHORT, :] * ca_ref[HALO_A:HALO_A + rows, :]
    for j in range(K_SHORT - 1):
        shift = K_SHORT - 1 - j
        conv_a += caw_ref[j:j + 1, :] * ca_ref[HALO_A - shift:HALO_A - shift + rows, :]
    y_ref[:, 0:D_CONV] = (z_ref[0] * conv_a).astype(bf16)

    for c0 in range(0, D_MODEL, MXU_COLS):
        o_ref[:, c0:c0 + MXU_COLS] = h_ref[:, c0:c0 + MXU_COLS] + jnp.dot(
            y_ref[...], wout_ref[:, c0:c0 + MXU_COLS], preferred_element_type=f32)

    ca_ref[0:HALO_A, :] = ca_ref[rows:rows + HALO_A, :]
    cb_ref[0:HALO_B, :] = cb_ref[rows:span, :]


def _mixer(h, norm_g, w_in, conv_a_w, conv_b_w, conv_b_bias, ln_g, ln_b, w_out):
    row = lambda i: (i, 0)
    return pl.pallas_call(
        _mixer_kernel,
        grid=(SEQ // MIX_ROWS,),
        in_specs=[
            pl.BlockSpec(memory_space=pltpu.SMEM),
            pl.BlockSpec((MIX_ROWS, D_MODEL), row),
            _resident((1, D_MODEL)),
            _resident(w_in.shape),
            _resident(conv_a_w.shape),
            _resident(conv_b_w.shape),
            _resident((1, D_CONV)),
            _resident((1, D_CONV)),
            _resident((1, D_CONV)),
            _resident(w_out.shape),
        ],
        out_specs=pl.BlockSpec((MIX_ROWS, D_MODEL), row),
        out_shape=jax.ShapeDtypeStruct((SEQ, D_MODEL), f32),
        scratch_shapes=[
            pltpu.VMEM((MIX_ROWS, D_MODEL), bf16),
            pltpu.VMEM((3, MIX_ROWS, D_CONV), f32),
            pltpu.VMEM((HALO_A + MIX_ROWS, D_CONV), f32),
            pltpu.VMEM((SUBLANES * (HALO_B + MIX_ROWS) + MIX_ROWS, D_CONV), f32),
            pltpu.VMEM((MIX_ROWS, 2 * D_CONV), bf16),
        ],
        compiler_params=pltpu.CompilerParams(
            dimension_semantics=("arbitrary",),
            vmem_limit_bytes=VMEM_LIMIT_BYTES),
        name="conv_mixer",
    )(jnp.zeros((1,), jnp.int32), h, norm_g, w_in, conv_a_w, conv_b_w, conv_b_bias, ln_g, ln_b, w_out)


def _mem_fold_kernel(mem_ref, g_ref, wq_ref, wk_ref, wv_ref, wo_ref, wqk_ref, vo_ref):
    mem_n = _rmsnorm(mem_ref[...], g_ref[...]).astype(bf16)
    k = jnp.dot(mem_n, wk_ref[...].astype(bf16), preferred_element_type=f32).astype(bf16)
    v = jnp.dot(mem_n, wv_ref[...].astype(bf16), preferred_element_type=f32).astype(bf16)
    wqk_ref[...] = lax.dot_general(wq_ref[...].astype(bf16), k, (((1,), (1,)), ((), ())),
                                   preferred_element_type=f32).astype(bf16)
    vo_ref[...] = jnp.dot(v, wo_ref[...].astype(bf16), preferred_element_type=f32).astype(bf16)


def _mem_fold(mem, norm_g, w_q, w_kv, w_o):
    hd = MEM_HEAD_DIM
    return pl.pallas_call(
        _mem_fold_kernel,
        grid=(N_MEM_HEADS,),
        in_specs=[
            pl.BlockSpec((MEM_LEN, D_MODEL), lambda h: (0, 0)),
            pl.BlockSpec((1, D_MODEL), lambda h: (0, 0)),
            pl.BlockSpec((D_MODEL, hd), lambda h: (0, h)),
            pl.BlockSpec((D_MODEL, hd), lambda h: (0, h)),
            pl.BlockSpec((D_MODEL, hd), lambda h: (0, h + N_MEM_HEADS)),
            pl.BlockSpec((hd, D_MODEL), lambda h: (h, 0)),
        ],
        out_specs=(pl.BlockSpec((D_MODEL, MEM_LEN), lambda h: (0, h)),
                   pl.BlockSpec((MEM_LEN, D_MODEL), lambda h: (h, 0))),
        out_shape=(jax.ShapeDtypeStruct((D_MODEL, N_MEM_HEADS * MEM_LEN), bf16),
                   jax.ShapeDtypeStruct((N_MEM_HEADS * MEM_LEN, D_MODEL), bf16)),
        compiler_params=pltpu.CompilerParams(
            dimension_semantics=("arbitrary",),
            vmem_limit_bytes=VMEM_LIMIT_BYTES),
        name="mem_fold",
    )(mem, norm_g, w_q, w_kv, w_kv, w_o)


def _attn_kernel(h_ref, g_ref, wqk_ref, vo_ref, o_ref, u_ref, p_ref):
    x = h_ref[...]
    u_ref[...] = _rmsnorm(x, g_ref[...]).astype(bf16)
    for hd in range(N_MEM_HEADS):
        cols = slice(hd * MEM_LEN, (hd + 1) * MEM_LEN)
        s = jnp.dot(u_ref[...], wqk_ref[:, cols], preferred_element_type=f32) * (MEM_HEAD_DIM ** -0.5)
        e = jnp.exp(s - jnp.max(s, axis=-1, keepdims=True))
        p_ref[:, cols] = (e / jnp.sum(e, axis=-1, keepdims=True)).astype(bf16)
    o_ref[...] = x + jnp.dot(p_ref[...], vo_ref[...], preferred_element_type=f32)


def _attn(h, norm_g, w_qk, vo):
    row = lambda i: (i, 0)
    return pl.pallas_call(
        _attn_kernel,
        grid=(SEQ // ATT_ROWS,),
        in_specs=[
            pl.BlockSpec((ATT_ROWS, D_MODEL), row),
            _resident((1, D_MODEL)),
            _resident(w_qk.shape),
            _resident(vo.shape),
        ],
        out_specs=pl.BlockSpec((ATT_ROWS, D_MODEL), row),
        out_shape=jax.ShapeDtypeStruct((SEQ, D_MODEL), f32),
        scratch_shapes=[pltpu.VMEM((ATT_ROWS, D_MODEL), bf16),
                        pltpu.VMEM((ATT_ROWS, N_MEM_HEADS * MEM_LEN), bf16)],
        compiler_params=pltpu.CompilerParams(
            dimension_semantics=("arbitrary",),
            vmem_limit_bytes=VMEM_LIMIT_BYTES),
        name="mem_attn",
    )(h, norm_g, w_qk, vo)


def kernel(x, mem, ffn1_norm, ffn1_w_gate_up, ffn1_w_down, mix_norm, w_mix_in, conv_a_w, conv_b_w,
           conv_b_bias, ln_b_gain, ln_b_bias, w_mix_out, mem_q_norm, mem_kv_norm, w_mem_q, w_mem_kv,
           w_mem_o, ffn2_norm, ffn2_w_gate_up, ffn2_w_down, final_norm):
    assert x.shape == (1, SEQ, D_MODEL) and mem.shape == (1, MEM_LEN, D_MODEL)
    assert ffn1_norm.shape[0] == 1, "one layer"
    h = x[0]
    final_g = final_norm[None, :]

    riders = _cast_riders(ffn2_w_gate_up[0], ffn2_w_down[0], w_mix_in[0], w_mix_out[0])
    h, (w2_gate_up, w2_down, w_in, w_out) = _ffn(
        h, ffn1_norm, ffn1_w_gate_up[0].astype(bf16), ffn1_w_down[0].astype(bf16), final_g,
        final_norm=False, riders=riders)
    h = _mixer(h, mix_norm, w_in, conv_a_w[0], conv_b_w[0], conv_b_bias, ln_b_gain, ln_b_bias, w_out)
    w_qk, vo = _mem_fold(mem[0], mem_kv_norm, w_mem_q[0], w_mem_kv[0], w_mem_o[0])
    h = _attn(h, mem_q_norm, w_qk, vo)
    h = _ffn_tiles(h, ffn2_norm, w2_gate_up, w2_down, final_g, final_norm=True)
    return h[None]
```

```python
import functools

import jax
import jax.numpy as jnp
from jax import lax
from jax.experimental import pallas as pl
from jax.experimental.pallas import tpu as pltpu

D_MODEL = 2048
SEQ = 16384
MEM_LEN = 256
D_FF = 5632
FFN_RES = 0.5
D_CONV = 1024
K_SHORT = 3
K_CONF = 31
N_MEM_HEADS = 4
MEM_HEAD_DIM = 512
RMS_EPS = 1e-6
LN_EPS = 1e-5

SUBLANES = 8
LANES = 128
MXU_COLS = 512
VMEM_LIMIT_BYTES = 60 * 1024 * 1024

FFN_ROWS = 1024
FFN_COLS = 512
MIX_ROWS = 256
ATT_ROWS = 1024
FOLD_PARTS = 4
HALO_A = SUBLANES
HALO_B = 32
CONV_ROW_CHUNKS = (32,) * 8
assert sum(CONV_ROW_CHUNKS) == MIX_ROWS and all(c % SUBLANES == 0 for c in CONV_ROW_CHUNKS)

f32 = jnp.float32
bf16 = jnp.bfloat16


def _rmsnorm(x, g):
    return x * lax.rsqrt(jnp.mean(x * x, axis=-1, keepdims=True) + RMS_EPS) * g


def _resident(shape):
    return pl.BlockSpec(shape, lambda *_: (0,) * len(shape), pipeline_mode=pl.Buffered(1))


def _ffn_kernel(x_ref, g_ref, wg_ref, wu_ref, wd_ref, fin_ref, *rest, final_norm, n_riders):
    cast_in, (o_ref, *cast_out), (u_ref,) = rest[:n_riders], rest[n_riders:-1], rest[-1:]
    j = pl.program_id(1)

    @pl.when(j == 0)
    def _():
        x = x_ref[...]
        u_ref[...] = _rmsnorm(x, g_ref[...]).astype(bf16)
        o_ref[...] = x

    u = u_ref[...]
    gate = jnp.dot(u, wg_ref[...], preferred_element_type=f32)
    up = jnp.dot(u, wu_ref[...], preferred_element_type=f32)
    act = (gate * jax.nn.sigmoid(gate) * (FFN_RES * up)).astype(bf16)
    o_ref[...] += jnp.dot(act, wd_ref[...], preferred_element_type=f32)

    for src, dst in zip(cast_in, cast_out):
        if len(dst.shape) == 3:
            for b in range(dst.shape[0]):
                dst[b] = src[:, b * FFN_COLS:(b + 1) * FFN_COLS].astype(bf16)
        else:
            dst[...] = src[...].astype(bf16)

    if final_norm:
        @pl.when(j == pl.num_programs(1) - 1)
        def _():
            o_ref[...] = _rmsnorm(o_ref[...], fin_ref[...])


def _gate_up_specs(w_gate_up):
    n_col = D_FF // FFN_COLS
    if w_gate_up.ndim == 3:
        return (pl.BlockSpec((None, D_MODEL, FFN_COLS), lambda i, j: (j, 0, 0)),
                pl.BlockSpec((None, D_MODEL, FFN_COLS), lambda i, j: (j + n_col, 0, 0)))
    return (pl.BlockSpec((D_MODEL, FFN_COLS), lambda i, j: (0, j)),
            pl.BlockSpec((D_MODEL, FFN_COLS), lambda i, j: (0, j + n_col)))


def _cast_riders(ffn_w_gate_up, ffn_w_down, w_mix_in, w_mix_out):
    n_row, n_col = SEQ // FFN_ROWS, D_FF // FFN_COLS
    gu_rows, gu_cols = D_MODEL // n_row, 2 * D_FF // n_col
    dn_rows = D_FF // (n_row * n_col)
    mi_cols, mo_cols = MXU_COLS, MXU_COLS // 2
    mi_steps, mo_steps = w_mix_in.shape[1] // mi_cols, w_mix_out.shape[1] // mo_cols
    assert gu_cols == 2 * FFN_COLS and dn_rows % 16 == 0 and mi_steps <= n_col and mo_steps <= n_col
    srcs = [ffn_w_gate_up, ffn_w_down, w_mix_in, w_mix_out]
    in_specs = [
        pl.BlockSpec((gu_rows, gu_cols), lambda i, j: (i, j)),
        pl.BlockSpec((dn_rows, D_MODEL), lambda i, j: (i * n_col + j, 0)),
        pl.BlockSpec((gu_rows, mi_cols), lambda i, j: (i, jnp.minimum(j, mi_steps - 1))),
        pl.BlockSpec((gu_rows, mo_cols), lambda i, j: (i, jnp.minimum(j, mo_steps - 1))),
    ]
    out_shapes = [
        jax.ShapeDtypeStruct((2 * n_col, D_MODEL, FFN_COLS), bf16),
        jax.ShapeDtypeStruct(ffn_w_down.shape, bf16),
        jax.ShapeDtypeStruct(w_mix_in.shape, bf16),
        jax.ShapeDtypeStruct(w_mix_out.shape, bf16),
    ]
    out_specs = [
        pl.BlockSpec((2, gu_rows, FFN_COLS), lambda i, j: (j, i, 0)),
        in_specs[1], in_specs[2], in_specs[3],
    ]
    return srcs, in_specs, out_shapes, out_specs


def _ffn(h, norm_g, w_gate_up, w_down, final_g, *, final_norm, riders=None):
    srcs, rider_in, rider_shapes, rider_out = riders if riders else ([], [], [], [])
    gate_spec, up_spec = _gate_up_specs(w_gate_up)
    outs = pl.pallas_call(
        functools.partial(_ffn_kernel, final_norm=final_norm, n_riders=len(srcs)),
        grid=(SEQ // FFN_ROWS, D_FF // FFN_COLS),
        in_specs=[
            pl.BlockSpec((FFN_ROWS, D_MODEL), lambda i, j: (i, 0)),
            pl.BlockSpec((1, D_MODEL), lambda i, j: (0, 0)),
            gate_spec,
            up_spec,
            pl.BlockSpec((FFN_COLS, D_MODEL), lambda i, j: (j, 0)),
            pl.BlockSpec((1, D_MODEL), lambda i, j: (0, 0)),
            *rider_in,
        ],
        out_specs=[pl.BlockSpec((FFN_ROWS, D_MODEL), lambda i, j: (i, 0)), *rider_out],
        out_shape=[jax.ShapeDtypeStruct((SEQ, D_MODEL), f32), *rider_shapes],
        scratch_shapes=[pltpu.VMEM((FFN_ROWS, D_MODEL), bf16)],
        compiler_params=pltpu.CompilerParams(
            dimension_semantics=("arbitrary", "arbitrary"),
            vmem_limit_bytes=VMEM_LIMIT_BYTES),
        name="ffn_final" if final_norm else "ffn",
    )(h, norm_g, w_gate_up, w_gate_up, w_down, final_g, *srcs)
    return outs[0], outs[1:]


def _mixer_kernel(zero_ref, h_ref, g_ref, win_ref, caw_ref, cbw_ref, cbb_ref, lng_ref, lnb_ref,
                  wout_ref, o_ref, u_ref, z_ref, ca_ref, cb_ref, y_ref):
    rows = h_ref.shape[0]
    span = HALO_B + rows
    cv0 = SUBLANES * span

    @pl.when(pl.program_id(0) == 0)
    def _():
        ca_ref[0:HALO_A, :] = jnp.zeros((HALO_A, D_CONV), f32)
        cb_ref[0:HALO_B, :] = jnp.zeros((HALO_B, D_CONV), f32)

    x = h_ref[...]
    u_ref[...] = _rmsnorm(x, g_ref[...]).astype(bf16)

    def proj(col0):
        return jnp.dot(u_ref[...], win_ref[:, col0:col0 + MXU_COLS], preferred_element_type=f32)

    for c0 in range(0, D_CONV, MXU_COLS):
        cb_ref[HALO_B:span, c0:c0 + MXU_COLS] = (
            proj(3 * D_CONV + c0) * jax.nn.sigmoid(proj(4 * D_CONV + c0)))
    for k in range(3):
        for c0 in range(0, D_CONV, MXU_COLS):
            z_ref[k, :, c0:c0 + MXU_COLS] = proj(k * D_CONV + c0)

    for r in range(1, SUBLANES):
        cb_ref[r * span + SUBLANES:(r + 1) * span, :] = cb_ref[SUBLANES - r:span - r, :]
    hidden0 = pl.multiple_of(zero_ref[0], SUBLANES)
    for c in range(D_CONV // LANES):
        cols = slice(c * LANES, (c + 1) * LANES)
        taps = [cbw_ref[j:j + 1, cols] for j in range(K_CONF)]
        bias = cbb_ref[:, cols]
        row0 = 0
        for chunk in CONV_ROW_CHUNKS:
            acc = jnp.broadcast_to(bias, (chunk, LANES))
            for j in range(K_CONF):
                q, r = divmod(K_CONF - 1 - j, SUBLANES)
                start = r * span + HALO_B + row0 - q * SUBLANES
                acc = acc + taps[j] * cb_ref[start:start + chunk, cols]
            cb_ref[pl.ds(hidden0 + cv0 + row0, chunk), cols] = acc
            row0 += chunk
    cv = cb_ref[cv0:cv0 + rows, :]
    mu = jnp.mean(cv, axis=-1, keepdims=True)
    cen = cv - mu
    var = jnp.mean(cen * cen, axis=-1, keepdims=True)
    ln = cen * lax.rsqrt(var + LN_EPS) * lng_ref[...] + lnb_ref[...]
    y_ref[:, D_CONV:2 * D_CONV] = (ln * jax.nn.sigmoid(ln)).astype(bf16)

    ca_ref[HALO_A:HALO_A + rows, :] = z_ref[1] * z_ref[2]
    conv_a = caw_ref[K_SHORT - 1:K_SHORT, :] * ca_ref[HALO_A:HALO_A + rows, :]
    for j in range(K_SHORT - 1):
        shift = K_SHORT - 1 - j
        conv_a += caw_ref[j:j + 1, :] * ca_ref[HALO_A - shift:HALO_A - shift + rows, :]
    y_ref[:, 0:D_CONV] = (z_ref[0] * conv_a).astype(bf16)

    for c0 in range(0, D_MODEL, MXU_COLS):
        o_ref[:, c0:c0 + MXU_COLS] = h_ref[:, c0:c0 + MXU_COLS] + jnp.dot(
            y_ref[...], wout_ref[:, c0:c0 + MXU_COLS], preferred_element_type=f32)

    ca_ref[0:HALO_A, :] = ca_ref[rows:rows + HALO_A, :]
    cb_ref[0:HALO_B, :] = cb_ref[rows:span, :]


def _mixer(h, norm_g, w_in, conv_a_w, conv_b_w, conv_b_bias, ln_g, ln_b, w_out):
    row = lambda i: (i, 0)
    return pl.pallas_call(
        _mixer_kernel,
        grid=(SEQ // MIX_ROWS,),
        in_specs=[
            pl.BlockSpec(memory_space=pltpu.SMEM),
            pl.BlockSpec((MIX_ROWS, D_MODEL), row),
            _resident((1, D_MODEL)),
            _resident(w_in.shape),
            _resident(conv_a_w.shape),
            _resident(conv_b_w.shape),
            _resident((1, D_CONV)),
            _resident((1, D_CONV)),
            _resident((1, D_CONV)),
            _resident(w_out.shape),
        ],
        out_specs=pl.BlockSpec((MIX_ROWS, D_MODEL), row),
        out_shape=jax.ShapeDtypeStruct((SEQ, D_MODEL), f32),
        scratch_shapes=[
            pltpu.VMEM((MIX_ROWS, D_MODEL), bf16),
            pltpu.VMEM((3, MIX_ROWS, D_CONV), f32),
            pltpu.VMEM((HALO_A + MIX_ROWS, D_CONV), f32),
            pltpu.VMEM((SUBLANES * (HALO_B + MIX_ROWS) + MIX_ROWS, D_CONV), f32),
            pltpu.VMEM((MIX_ROWS, 2 * D_CONV), bf16),
        ],
        compiler_params=pltpu.CompilerParams(
            dimension_semantics=("arbitrary",),
            vmem_limit_bytes=VMEM_LIMIT_BYTES),
        name="conv_mixer",
    )(jnp.zeros((1,), jnp.int32), h, norm_g, w_in, conv_a_w, conv_b_w, conv_b_bias, ln_g, ln_b, w_out)


def _mem_fold_kernel(mem_ref, g_ref, wq_ref, wk_ref, wv_ref, wo_ref, wqk_ref, vo_ref,
                     memn_ref, k_ref, v_ref):
    head, part = pl.program_id(0), pl.program_id(1)

    @pl.when((head == 0) & (part == 0))
    def _():
        memn_ref[...] = _rmsnorm(mem_ref[...], g_ref[...]).astype(bf16)

    @pl.when(part == 0)
    def _():
        mem_n = memn_ref[...]
        k_ref[...] = jnp.dot(mem_n, wk_ref[...].astype(bf16), preferred_element_type=f32).astype(bf16)
        v_ref[...] = jnp.dot(mem_n, wv_ref[...].astype(bf16), preferred_element_type=f32).astype(bf16)

    wqk_ref[...] = lax.dot_general(wq_ref[...].astype(bf16), k_ref[...], (((1,), (1,)), ((), ())),
                                   preferred_element_type=f32).astype(bf16)
    vo_ref[...] = jnp.dot(v_ref[...], wo_ref[...].astype(bf16),
                          preferred_element_type=f32).astype(bf16)


def _mem_fold(mem, norm_g, w_q, w_kv, w_o):
    hd, blk = MEM_HEAD_DIM, D_MODEL // FOLD_PARTS
    return pl.pallas_call(
        _mem_fold_kernel,
        grid=(N_MEM_HEADS, FOLD_PARTS),
        in_specs=[
            pl.BlockSpec((MEM_LEN, D_MODEL), lambda h, p: (0, 0)),
            pl.BlockSpec((1, D_MODEL), lambda h, p: (0, 0)),
            pl.BlockSpec((blk, hd), lambda h, p: (p, h)),
            pl.BlockSpec((D_MODEL, hd), lambda h, p: (0, h)),
            pl.BlockSpec((D_MODEL, hd), lambda h, p: (0, h + N_MEM_HEADS)),
            pl.BlockSpec((hd, blk), lambda h, p: (h, p)),
        ],
        out_specs=(pl.BlockSpec((blk, MEM_LEN), lambda h, p: (p, h)),
                   pl.BlockSpec((MEM_LEN, blk), lambda h, p: (h, p))),
        out_shape=(jax.ShapeDtypeStruct((D_MODEL, N_MEM_HEADS * MEM_LEN), bf16),
                   jax.ShapeDtypeStruct((N_MEM_HEADS * MEM_LEN, D_MODEL), bf16)),
        scratch_shapes=[pltpu.VMEM((MEM_LEN, D_MODEL), bf16),
                        pltpu.VMEM((MEM_LEN, hd), bf16),
                        pltpu.VMEM((MEM_LEN, hd), bf16)],
        compiler_params=pltpu.CompilerParams(
            dimension_semantics=("arbitrary", "arbitrary"),
            vmem_limit_bytes=VMEM_LIMIT_BYTES),
        name="mem_fold",
    )(mem, norm_g, w_q, w_kv, w_kv, w_o)


def _attn_kernel(h_ref, g_ref, wqk_ref, vo_ref, o_ref, u_ref, p_ref):
    x = h_ref[...]
    u_ref[...] = _rmsnorm(x, g_ref[...]).astype(bf16)
    for hd in range(N_MEM_HEADS):
        cols = slice(hd * MEM_LEN, (hd + 1) * MEM_LEN)
        s = jnp.dot(u_ref[...], wqk_ref[:, cols], preferred_element_type=f32) * (MEM_HEAD_DIM ** -0.5)
        e = jnp.exp(s - jnp.max(s, axis=-1, keepdims=True))
        p_ref[:, cols] = (e / jnp.sum(e, axis=-1, keepdims=True)).astype(bf16)
    o_ref[...] = x + jnp.dot(p_ref[...], vo_ref[...], preferred_element_type=f32)


def _attn(h, norm_g, w_qk, vo):
    row = lambda i: (i, 0)
    return pl.pallas_call(
        _attn_kernel,
        grid=(SEQ // ATT_ROWS,),
        in_specs=[
            pl.BlockSpec((ATT_ROWS, D_MODEL), row),
            _resident((1, D_MODEL)),
            _resident(w_qk.shape),
            _resident(vo.shape),
        ],
        out_specs=pl.BlockSpec((ATT_ROWS, D_MODEL), row),
        out_shape=jax.ShapeDtypeStruct((SEQ, D_MODEL), f32),
        scratch_shapes=[pltpu.VMEM((ATT_ROWS, D_MODEL), bf16),
                        pltpu.VMEM((ATT_ROWS, N_MEM_HEADS * MEM_LEN), bf16)],
        compiler_params=pltpu.CompilerParams(
            dimension_semantics=("arbitrary",),
            vmem_limit_bytes=VMEM_LIMIT_BYTES),
        name="mem_attn",
    )(h, norm_g, w_qk, vo)


def kernel(x, mem, ffn1_norm, ffn1_w_gate_up, ffn1_w_down, mix_norm, w_mix_in, conv_a_w, conv_b_w,
           conv_b_bias, ln_b_gain, ln_b_bias, w_mix_out, mem_q_norm, mem_kv_norm, w_mem_q, w_mem_kv,
           w_mem_o, ffn2_norm, ffn2_w_gate_up, ffn2_w_down, final_norm):
    assert x.shape == (1, SEQ, D_MODEL) and mem.shape == (1, MEM_LEN, D_MODEL)
    assert ffn1_norm.shape[0] == 1, "one layer"
    h = x[0]
    final_g = final_norm[None, :]

    riders = _cast_riders(ffn2_w_gate_up[0], ffn2_w_down[0], w_mix_in[0], w_mix_out[0])
    h, (w2_gate_up, w2_down, w_in, w_out) = _ffn(
        h, ffn1_norm, ffn1_w_gate_up[0].astype(bf16), ffn1_w_down[0].astype(bf16), final_g,
        final_norm=False, riders=riders)
    h = _mixer(h, mix_norm, w_in, conv_a_w[0], conv_b_w[0], conv_b_bias, ln_b_gain, ln_b_bias, w_out)
    w_qk, vo = _mem_fold(mem[0], mem_kv_norm, w_mem_q[0], w_mem_kv[0], w_mem_o[0])
    h = _attn(h, mem_q_norm, w_qk, vo)
    h, _ = _ffn(h, ffn2_norm, w2_gate_up, w2_down, final_g, final_norm=True)
    return h[None]
```

```python
import functools

import jax
import jax.numpy as jnp
from jax import lax
from jax.experimental import pallas as pl
from jax.experimental.pallas import tpu as pltpu

D_MODEL = 2048
SEQ = 16384
MEM_LEN = 256
D_FF = 5632
FFN_RES = 0.5
D_CONV = 1024
K_SHORT = 3
K_CONF = 31
N_MEM_HEADS = 4
MEM_HEAD_DIM = 512
RMS_EPS = 1e-6
LN_EPS = 1e-5

SUBLANES = 8
LANES = 128
MXU_COLS = 512
VMEM_LIMIT_BYTES = 60 * 1024 * 1024

FFN_ROWS = 1024
FFN_COLS = 512
MIX_ROWS = 256
ATT_ROWS = 1024
HALO_A = SUBLANES
HALO_B = 32
CONV_ROW_CHUNKS = (32,) * 8
assert sum(CONV_ROW_CHUNKS) == MIX_ROWS and all(c % SUBLANES == 0 for c in CONV_ROW_CHUNKS)

f32 = jnp.float32
bf16 = jnp.bfloat16


def _rmsnorm(x, g):
    return x * lax.rsqrt(jnp.mean(x * x, axis=-1, keepdims=True) + RMS_EPS) * g


def _resident(shape):
    return pl.BlockSpec(shape, lambda *_: (0,) * len(shape), pipeline_mode=pl.Buffered(1))


def _ffn_kernel(x_ref, g_ref, wg_ref, wu_ref, wd_ref, fin_ref, *rest, final_norm, n_riders):
    cast_in, (o_ref, *cast_out), (u_ref,) = rest[:n_riders], rest[n_riders:-1], rest[-1:]
    j = pl.program_id(1)
    last = pl.num_programs(1) - 1

    def step(first, final):
        if first:
            u_ref[...] = _rmsnorm(x_ref[...], g_ref[...]).astype(bf16)
        u = u_ref[...]
        gate = jnp.dot(u, wg_ref[...], preferred_element_type=f32)
        up = jnp.dot(u, wu_ref[...], preferred_element_type=f32)
        act = (gate * jax.nn.sigmoid(gate) * (FFN_RES * up)).astype(bf16)
        acc = (x_ref if first else o_ref)[...] + jnp.dot(act, wd_ref[...], preferred_element_type=f32)
        o_ref[...] = _rmsnorm(acc, fin_ref[...]) if final else acc

        for src, dst in zip(cast_in, cast_out):
            if len(dst.shape) == 3:
                for b in range(dst.shape[0]):
                    dst[b] = src[:, b * FFN_COLS:(b + 1) * FFN_COLS].astype(bf16)
            else:
                dst[...] = src[...].astype(bf16)

    pl.when(j == 0)(functools.partial(step, True, False))
    if final_norm:
        pl.when((j > 0) & (j < last))(functools.partial(step, False, False))
        pl.when(j == last)(functools.partial(step, False, True))
    else:
        pl.when(j > 0)(functools.partial(step, False, False))


def _gate_up_specs(w_gate_up):
    n_col = D_FF // FFN_COLS
    if w_gate_up.ndim == 3:
        return (pl.BlockSpec((None, D_MODEL, FFN_COLS), lambda i, j: (j, 0, 0)),
                pl.BlockSpec((None, D_MODEL, FFN_COLS), lambda i, j: (j + n_col, 0, 0)))
    return (pl.BlockSpec((D_MODEL, FFN_COLS), lambda i, j: (0, j)),
            pl.BlockSpec((D_MODEL, FFN_COLS), lambda i, j: (0, j + n_col)))


def _cast_riders(ffn_w_gate_up, ffn_w_down, w_mix_in, w_mix_out):
    n_row, n_col = SEQ // FFN_ROWS, D_FF // FFN_COLS
    gu_rows, gu_cols = D_MODEL // n_row, 2 * D_FF // n_col
    dn_rows = D_FF // (n_row * n_col)
    mi_cols, mo_cols = MXU_COLS, MXU_COLS // 2
    mi_steps, mo_steps = w_mix_in.shape[1] // mi_cols, w_mix_out.shape[1] // mo_cols
    assert gu_cols == 2 * FFN_COLS and dn_rows % 16 == 0 and mi_steps <= n_col and mo_steps <= n_col
    srcs = [ffn_w_gate_up, ffn_w_down, w_mix_in, w_mix_out]
    in_specs = [
        pl.BlockSpec((gu_rows, gu_cols), lambda i, j: (i, j)),
        pl.BlockSpec((dn_rows, D_MODEL), lambda i, j: (i * n_col + j, 0)),
        pl.BlockSpec((gu_rows, mi_cols), lambda i, j: (i, jnp.minimum(j, mi_steps - 1))),
        pl.BlockSpec((gu_rows, mo_cols), lambda i, j: (i, jnp.minimum(j, mo_steps - 1))),
    ]
    out_shapes = [
        jax.ShapeDtypeStruct((2 * n_col, D_MODEL, FFN_COLS), bf16),
        jax.ShapeDtypeStruct(ffn_w_down.shape, bf16),
        jax.ShapeDtypeStruct(w_mix_in.shape, bf16),
        jax.ShapeDtypeStruct(w_mix_out.shape, bf16),
    ]
    out_specs = [
        pl.BlockSpec((2, gu_rows, FFN_COLS), lambda i, j: (j, i, 0)),
        in_specs[1], in_specs[2], in_specs[3],
    ]
    return srcs, in_specs, out_shapes, out_specs


def _ffn(h, norm_g, w_gate_up, w_down, final_g, *, final_norm, riders=None):
    srcs, rider_in, rider_shapes, rider_out = riders if riders else ([], [], [], [])
    gate_spec, up_spec = _gate_up_specs(w_gate_up)
    outs = pl.pallas_call(
        functools.partial(_ffn_kernel, final_norm=final_norm, n_riders=len(srcs)),
        grid=(SEQ // FFN_ROWS, D_FF // FFN_COLS),
        in_specs=[
            pl.BlockSpec((FFN_ROWS, D_MODEL), lambda i, j: (i, 0)),
            pl.BlockSpec((1, D_MODEL), lambda i, j: (0, 0)),
            gate_spec,
            up_spec,
            pl.BlockSpec((FFN_COLS, D_MODEL), lambda i, j: (j, 0)),
            pl.BlockSpec((1, D_MODEL), lambda i, j: (0, 0)),
            *rider_in,
        ],
        out_specs=[pl.BlockSpec((FFN_ROWS, D_MODEL), lambda i, j: (i, 0)), *rider_out],
        out_shape=[jax.ShapeDtypeStruct((SEQ, D_MODEL), f32), *rider_shapes],
        scratch_shapes=[pltpu.VMEM((FFN_ROWS, D_MODEL), bf16)],
        compiler_params=pltpu.CompilerParams(
            dimension_semantics=("arbitrary", "arbitrary"),
            vmem_limit_bytes=VMEM_LIMIT_BYTES),
        name="ffn_final" if final_norm else "ffn",
    )(h, norm_g, w_gate_up, w_gate_up, w_down, final_g, *srcs)
    return outs[0], outs[1:]


def _mixer_kernel(zero_ref, h_ref, g_ref, win_ref, caw_ref, cbw_ref, cbb_ref, lng_ref, lnb_ref,
                  wout_ref, o_ref, u_ref, z_ref, ca_ref, cb_ref, y_ref):
    rows = h_ref.shape[0]
    span = HALO_B + rows
    cv0 = SUBLANES * span

    @pl.when(pl.program_id(0) == 0)
    def _():
        ca_ref[0:HALO_A, :] = jnp.zeros((HALO_A, D_CONV), f32)
        cb_ref[0:HALO_B, :] = jnp.zeros((HALO_B, D_CONV), f32)

    x = h_ref[...]
    u_ref[...] = _rmsnorm(x, g_ref[...]).astype(bf16)

    def proj(col0):
        return jnp.dot(u_ref[...], win_ref[:, col0:col0 + MXU_COLS], preferred_element_type=f32)

    for c0 in range(0, D_CONV, MXU_COLS):
        cb_ref[HALO_B:span, c0:c0 + MXU_COLS] = (
            proj(3 * D_CONV + c0) * jax.nn.sigmoid(proj(4 * D_CONV + c0)))
    for k in range(3):
        for c0 in range(0, D_CONV, MXU_COLS):
            z_ref[k, :, c0:c0 + MXU_COLS] = proj(k * D_CONV + c0)

    for r in range(1, SUBLANES):
        cb_ref[r * span + SUBLANES:(r + 1) * span, :] = cb_ref[SUBLANES - r:span - r, :]
    hidden0 = pl.multiple_of(zero_ref[0], SUBLANES)
    for c in range(D_CONV // LANES):
        cols = slice(c * LANES, (c + 1) * LANES)
        taps = [cbw_ref[j:j + 1, cols] for j in range(K_CONF)]
        bias = cbb_ref[:, cols]
        row0 = 0
        for chunk in CONV_ROW_CHUNKS:
            acc = jnp.broadcast_to(bias, (chunk, LANES))
            for j in range(K_CONF):
                q, r = divmod(K_CONF - 1 - j, SUBLANES)
                start = r * span + HALO_B + row0 - q * SUBLANES
                acc = acc + taps[j] * cb_ref[start:start + chunk, cols]
            cb_ref[pl.ds(hidden0 + cv0 + row0, chunk), cols] = acc
            row0 += chunk
    cv = cb_ref[cv0:cv0 + rows, :]
    mu = jnp.mean(cv, axis=-1, keepdims=True)
    cen = cv - mu
    var = jnp.mean(cen * cen, axis=-1, keepdims=True)
    ln = cen * lax.rsqrt(var + LN_EPS) * lng_ref[...] + lnb_ref[...]
    y_ref[:, D_CONV:2 * D_CONV] = (ln * jax.nn.sigmoid(ln)).astype(bf16)

    ca_ref[HALO_A:HALO_A + rows, :] = z_ref[1] * z_ref[2]
    conv_a = caw_ref[K_SHORT - 1:K_SHORT, :] * ca_ref[HALO_A:HALO_A + rows, :]
    for j in range(K_SHORT - 1):
        shift = K_SHORT - 1 - j
        conv_a += caw_ref[j:j + 1, :] * ca_ref[HALO_A - shift:HALO_A - shift + rows, :]
    y_ref[:, 0:D_CONV] = (z_ref[0] * conv_a).astype(bf16)

    for c0 in range(0, D_MODEL, MXU_COLS):
        o_ref[:, c0:c0 + MXU_COLS] = h_ref[:, c0:c0 + MXU_COLS] + jnp.dot(
            y_ref[...], wout_ref[:, c0:c0 + MXU_COLS], preferred_element_type=f32)

    ca_ref[0:HALO_A, :] = ca_ref[rows:rows + HALO_A, :]
    cb_ref[0:HALO_B, :] = cb_ref[rows:span, :]


def _mixer(h, norm_g, w_in, conv_a_w, conv_b_w, conv_b_bias, ln_g, ln_b, w_out):
    row = lambda i: (i, 0)
    return pl.pallas_call(
        _mixer_kernel,
        grid=(SEQ // MIX_ROWS,),
        in_specs=[
            pl.BlockSpec(memory_space=pltpu.SMEM),
            pl.BlockSpec((MIX_ROWS, D_MODEL), row),
            _resident((1, D_MODEL)),
            _resident(w_in.shape),
            _resident(conv_a_w.shape),
            _resident(conv_b_w.shape),
            _resident((1, D_CONV)),
            _resident((1, D_CONV)),
            _resident((1, D_CONV)),
            _resident(w_out.shape),
        ],
        out_specs=pl.BlockSpec((MIX_ROWS, D_MODEL), row),
        out_shape=jax.ShapeDtypeStruct((SEQ, D_MODEL), f32),
        scratch_shapes=[
            pltpu.VMEM((MIX_ROWS, D_MODEL), bf16),
            pltpu.VMEM((3, MIX_ROWS, D_CONV), f32),
            pltpu.VMEM((HALO_A + MIX_ROWS, D_CONV), f32),
            pltpu.VMEM((SUBLANES * (HALO_B + MIX_ROWS) + MIX_ROWS, D_CONV), f32),
            pltpu.VMEM((MIX_ROWS, 2 * D_CONV), bf16),
        ],
        compiler_params=pltpu.CompilerParams(
            dimension_semantics=("arbitrary",),
            vmem_limit_bytes=VMEM_LIMIT_BYTES),
        name="conv_mixer",
    )(jnp.zeros((1,), jnp.int32), h, norm_g, w_in, conv_a_w, conv_b_w, conv_b_bias, ln_g, ln_b, w_out)


def _mem_fold_kernel(mem_ref, g_ref, wq_ref, wk_ref, wv_ref, wo_ref, wqk_ref, vo_ref):
    mem_n = _rmsnorm(mem_ref[...], g_ref[...]).astype(bf16)
    k = jnp.dot(mem_n, wk_ref[...].astype(bf16), preferred_element_type=f32).astype(bf16)
    v = jnp.dot(mem_n, wv_ref[...].astype(bf16), preferred_element_type=f32).astype(bf16)
    wqk_ref[...] = lax.dot_general(wq_ref[...].astype(bf16), k, (((1,), (1,)), ((), ())),
                                   preferred_element_type=f32).astype(bf16)
    vo_ref[...] = jnp.dot(v, wo_ref[...].astype(bf16), preferred_element_type=f32).astype(bf16)


def _mem_fold(mem, norm_g, w_q, w_kv, w_o):
    hd = MEM_HEAD_DIM
    return pl.pallas_call(
        _mem_fold_kernel,
        grid=(N_MEM_HEADS,),
        in_specs=[
            pl.BlockSpec((MEM_LEN, D_MODEL), lambda h: (0, 0)),
            pl.BlockSpec((1, D_MODEL), lambda h: (0, 0)),
            pl.BlockSpec((D_MODEL, hd), lambda h: (0, h)),
            pl.BlockSpec((D_MODEL, hd), lambda h: (0, h)),
            pl.BlockSpec((D_MODEL, hd), lambda h: (0, h + N_MEM_HEADS)),
            pl.BlockSpec((hd, D_MODEL), lambda h: (h, 0)),
        ],
        out_specs=(pl.BlockSpec((D_MODEL, MEM_LEN), lambda h: (0, h)),
                   pl.BlockSpec((MEM_LEN, D_MODEL), lambda h: (h, 0))),
        out_shape=(jax.ShapeDtypeStruct((D_MODEL, N_MEM_HEADS * MEM_LEN), bf16),
                   jax.ShapeDtypeStruct((N_MEM_HEADS * MEM_LEN, D_MODEL), bf16)),
        compiler_params=pltpu.CompilerParams(
            dimension_semantics=("arbitrary",),
            vmem_limit_bytes=VMEM_LIMIT_BYTES),
        name="mem_fold",
    )(mem, norm_g, w_q, w_kv, w_kv, w_o)


def _attn_kernel(h_ref, g_ref, wqk_ref, vo_ref, o_ref, u_ref, p_ref):
    x = h_ref[...]
    u_ref[...] = _rmsnorm(x, g_ref[...]).astype(bf16)
    for hd in range(N_MEM_HEADS):
        cols = slice(hd * MEM_LEN, (hd + 1) * MEM_LEN)
        s = jnp.dot(u_ref[...], wqk_ref[:, cols], preferred_element_type=f32) * (MEM_HEAD_DIM ** -0.5)
        e = jnp.exp(s - jnp.max(s, axis=-1, keepdims=True))
        p_ref[:, cols] = (e / jnp.sum(e, axis=-1, keepdims=True)).astype(bf16)
    o_ref[...] = x + jnp.dot(p_ref[...], vo_ref[...], preferred_element_type=f32)


def _attn(h, norm_g, w_qk, vo):
    row = lambda i: (i, 0)
    return pl.pallas_call(
        _attn_kernel,
        grid=(SEQ // ATT_ROWS,),
        in_specs=[
            pl.BlockSpec((ATT_ROWS, D_MODEL), row),
            _resident((1, D_MODEL)),
            _resident(w_qk.shape),
            _resident(vo.shape),
        ],
        out_specs=pl.BlockSpec((ATT_ROWS, D_MODEL), row),
        out_shape=jax.ShapeDtypeStruct((SEQ, D_MODEL), f32),
        scratch_shapes=[pltpu.VMEM((ATT_ROWS, D_MODEL), bf16),
                        pltpu.VMEM((ATT_ROWS, N_MEM_HEADS * MEM_LEN), bf16)],
        compiler_params=pltpu.CompilerParams(
            dimension_semantics=("arbitrary",),
            vmem_limit_bytes=VMEM_LIMIT_BYTES),
        name="mem_attn",
    )(h, norm_g, w_qk, vo)


def kernel(x, mem, ffn1_norm, ffn1_w_gate_up, ffn1_w_down, mix_norm, w_mix_in, conv_a_w, conv_b_w,
           conv_b_bias, ln_b_gain, ln_b_bias, w_mix_out, mem_q_norm, mem_kv_norm, w_mem_q, w_mem_kv,
           w_mem_o, ffn2_norm, ffn2_w_gate_up, ffn2_w_down, final_norm):
    assert x.shape == (1, SEQ, D_MODEL) and mem.shape == (1, MEM_LEN, D_MODEL)
    assert ffn1_norm.shape[0] == 1, "one layer"
    h = x[0]
    final_g = final_norm[None, :]

    riders = _cast_riders(ffn2_w_gate_up[0], ffn2_w_down[0], w_mix_in[0], w_mix_out[0])
    h, (w2_gate_up, w2_down, w_in, w_out) = _ffn(
        h, ffn1_norm, ffn1_w_gate_up[0].astype(bf16), ffn1_w_down[0].astype(bf16), final_g,
        final_norm=False, riders=riders)
    h = _mixer(h, mix_norm, w_in, conv_a_w[0], conv_b_w[0], conv_b_bias, ln_b_gain, ln_b_bias, w_out)
    w_qk, vo = _mem_fold(mem[0], mem_kv_norm, w_mem_q[0], w_mem_kv[0], w_mem_o[0])
    h = _attn(h, mem_q_norm, w_qk, vo)
    h, _ = _ffn(h, ffn2_norm, w2_gate_up, w2_down, final_g, final_norm=True)
    return h[None]
```

```python
import functools

import jax
import jax.numpy as jnp
from jax import lax
from jax.experimental import pallas as pl
from jax.experimental.pallas import tpu as pltpu

D_MODEL = 2048
SEQ = 16384
MEM_LEN = 256
D_FF = 5632
FFN_RES = 0.5
D_CONV = 1024
K_SHORT = 3
K_CONF = 31
N_MEM_HEADS = 4
MEM_HEAD_DIM = 512
RMS_EPS = 1e-6
LN_EPS = 1e-5

SUBLANES = 8
LANES = 128
MXU_COLS = 512
VMEM_LIMIT_BYTES = 60 * 1024 * 1024

FFN_ROWS = 1024
FFN_COLS = 512
MIX_ROWS = 256
ATT_ROWS = 1024
HALO_A = SUBLANES
HALO_B = 32
CONV_ROW_CHUNKS = (32,) * 8
assert sum(CONV_ROW_CHUNKS) == MIX_ROWS and all(c % SUBLANES == 0 for c in CONV_ROW_CHUNKS)

f32 = jnp.float32
bf16 = jnp.bfloat16


def _rmsnorm(x, g):
    return x * lax.rsqrt(jnp.mean(x * x, axis=-1, keepdims=True) + RMS_EPS) * g


def _resident(shape):
    return pl.BlockSpec(shape, lambda *_: (0,) * len(shape), pipeline_mode=pl.Buffered(1))


def _ffn_kernel(x_ref, g_ref, wg_ref, wu_ref, wd_ref, fin_ref, *rest, final_norm, n_riders):
    cast_in, (o_ref, *cast_out), (u_ref,) = rest[:n_riders], rest[n_riders:-1], rest[-1:]
    j = pl.program_id(1)
    last = pl.num_programs(1) - 1

    def step(first, final):
        if first:
            u_ref[...] = _rmsnorm(x_ref[...], g_ref[...]).astype(bf16)
        u = u_ref[...]
        halves = []
        for c0 in range(0, FFN_COLS, FFN_COLS // 2):
            cc = slice(c0, c0 + FFN_COLS // 2)
            gate = jnp.dot(u, wg_ref[:, cc], preferred_element_type=f32)
            up = jnp.dot(u, wu_ref[:, cc], preferred_element_type=f32)
            halves.append((gate * jax.nn.sigmoid(gate) * (FFN_RES * up)).astype(bf16))
        act = jnp.concatenate(halves, axis=1)
        acc = (x_ref if first else o_ref)[...] + jnp.dot(act, wd_ref[...], preferred_element_type=f32)
        o_ref[...] = _rmsnorm(acc, fin_ref[...]) if final else acc

        for src, dst in zip(cast_in, cast_out):
            if len(dst.shape) == 3:
                for b in range(dst.shape[0]):
                    dst[b] = src[:, b * FFN_COLS:(b + 1) * FFN_COLS].astype(bf16)
            else:
                dst[...] = src[...].astype(bf16)

    pl.when(j == 0)(functools.partial(step, True, False))
    if final_norm:
        pl.when((j > 0) & (j < last))(functools.partial(step, False, False))
        pl.when(j == last)(functools.partial(step, False, True))
    else:
        pl.when(j > 0)(functools.partial(step, False, False))


def _gate_up_specs(w_gate_up):
    n_col = D_FF // FFN_COLS
    if w_gate_up.ndim == 3:
        return (pl.BlockSpec((None, D_MODEL, FFN_COLS), lambda i, j: (j, 0, 0)),
                pl.BlockSpec((None, D_MODEL, FFN_COLS), lambda i, j: (j + n_col, 0, 0)))
    return (pl.BlockSpec((D_MODEL, FFN_COLS), lambda i, j: (0, j)),
            pl.BlockSpec((D_MODEL, FFN_COLS), lambda i, j: (0, j + n_col)))


def _cast_riders(ffn_w_gate_up, ffn_w_down, w_mix_in, w_mix_out):
    n_row, n_col = SEQ // FFN_ROWS, D_FF // FFN_COLS
    gu_rows, gu_cols = D_MODEL // n_row, 2 * D_FF // n_col
    dn_rows = D_FF // (n_row * n_col)
    mi_cols, mo_cols = MXU_COLS, MXU_COLS // 2
    mi_steps, mo_steps = w_mix_in.shape[1] // mi_cols, w_mix_out.shape[1] // mo_cols
    assert gu_cols == 2 * FFN_COLS and dn_rows % 16 == 0 and mi_steps <= n_col and mo_steps <= n_col
    srcs = [ffn_w_gate_up, ffn_w_down, w_mix_in, w_mix_out]
    in_specs = [
        pl.BlockSpec((gu_rows, gu_cols), lambda i, j: (i, j)),
        pl.BlockSpec((dn_rows, D_MODEL), lambda i, j: (i * n_col + j, 0)),
        pl.BlockSpec((gu_rows, mi_cols), lambda i, j: (i, jnp.minimum(j, mi_steps - 1))),
        pl.BlockSpec((gu_rows, mo_cols), lambda i, j: (i, jnp.minimum(j, mo_steps - 1))),
    ]
    out_shapes = [
        jax.ShapeDtypeStruct((2 * n_col, D_MODEL, FFN_COLS), bf16),
        jax.ShapeDtypeStruct(ffn_w_down.shape, bf16),
        jax.ShapeDtypeStruct(w_mix_in.shape, bf16),
        jax.ShapeDtypeStruct(w_mix_out.shape, bf16),
    ]
    out_specs = [
        pl.BlockSpec((2, gu_rows, FFN_COLS), lambda i, j: (j, i, 0)),
        in_specs[1], in_specs[2], in_specs[3],
    ]
    return srcs, in_specs, out_shapes, out_specs


def _ffn(h, norm_g, w_gate_up, w_down, final_g, *, final_norm, riders=None):
    srcs, rider_in, rider_shapes, rider_out = riders if riders else ([], [], [], [])
    gate_spec, up_spec = _gate_up_specs(w_gate_up)
    outs = pl.pallas_call(
        functools.partial(_ffn_kernel, final_norm=final_norm, n_riders=len(srcs)),
        grid=(SEQ // FFN_ROWS, D_FF // FFN_COLS),
        in_specs=[
            pl.BlockSpec((FFN_ROWS, D_MODEL), lambda i, j: (i, 0)),
            pl.BlockSpec((1, D_MODEL), lambda i, j: (0, 0)),
            gate_spec,
            up_spec,
            pl.BlockSpec((FFN_COLS, D_MODEL), lambda i, j: (j, 0)),
            pl.BlockSpec((1, D_MODEL), lambda i, j: (0, 0)),
            *rider_in,
        ],
        out_specs=[pl.BlockSpec((FFN_ROWS, D_MODEL), lambda i, j: (i, 0)), *rider_out],
        out_shape=[jax.ShapeDtypeStruct((SEQ, D_MODEL), f32), *rider_shapes],
        scratch_shapes=[pltpu.VMEM((FFN_ROWS, D_MODEL), bf16)],
        compiler_params=pltpu.CompilerParams(
            dimension_semantics=("arbitrary", "arbitrary"),
            vmem_limit_bytes=VMEM_LIMIT_BYTES),
        name="ffn_final" if final_norm else "ffn",
    )(h, norm_g, w_gate_up, w_gate_up, w_down, final_g, *srcs)
    return outs[0], outs[1:]


def _mixer_kernel(zero_ref, h_ref, g_ref, win_ref, caw_ref, cbw_ref, cbb_ref, lng_ref, lnb_ref,
                  wout_ref, o_ref, u_ref, z_ref, ca_ref, cb_ref, y_ref):
    rows = h_ref.shape[0]
    span = HALO_B + rows
    cv0 = SUBLANES * span

    @pl.when(pl.program_id(0) == 0)
    def _():
        ca_ref[0:HALO_A, :] = jnp.zeros((HALO_A, D_CONV), f32)
        cb_ref[0:HALO_B, :] = jnp.zeros((HALO_B, D_CONV), f32)

    x = h_ref[...]
    u_ref[...] = _rmsnorm(x, g_ref[...]).astype(bf16)

    def proj(col0):
        return jnp.dot(u_ref[...], win_ref[:, col0:col0 + MXU_COLS], preferred_element_type=f32)

    for c0 in range(0, D_CONV, MXU_COLS):
        cb_ref[HALO_B:span, c0:c0 + MXU_COLS] = (
            proj(3 * D_CONV + c0) * jax.nn.sigmoid(proj(4 * D_CONV + c0)))
    for k in range(3):
        for c0 in range(0, D_CONV, MXU_COLS):
            z_ref[k, :, c0:c0 + MXU_COLS] = proj(k * D_CONV + c0)

    for r in range(1, SUBLANES):
        cb_ref[r * span + SUBLANES:(r + 1) * span, :] = cb_ref[SUBLANES - r:span - r, :]
    hidden0 = pl.multiple_of(zero_ref[0], SUBLANES)
    for c in range(D_CONV // LANES):
        cols = slice(c * LANES, (c + 1) * LANES)
        taps = [cbw_ref[j:j + 1, cols] for j in range(K_CONF)]
        bias = cbb_ref[:, cols]
        row0 = 0
        for chunk in CONV_ROW_CHUNKS:
            acc = jnp.broadcast_to(bias, (chunk, LANES))
            for j in range(K_CONF):
                q, r = divmod(K_CONF - 1 - j, SUBLANES)
                start = r * span + HALO_B + row0 - q * SUBLANES
                acc = acc + taps[j] * cb_ref[start:start + chunk, cols]
            cb_ref[pl.ds(hidden0 + cv0 + row0, chunk), cols] = acc
            row0 += chunk
    cv = cb_ref[cv0:cv0 + rows, :]
    mu = jnp.mean(cv, axis=-1, keepdims=True)
    cen = cv - mu
    var = jnp.mean(cen * cen, axis=-1, keepdims=True)
    ln = cen * lax.rsqrt(var + LN_EPS) * lng_ref[...] + lnb_ref[...]
    y_ref[:, D_CONV:2 * D_CONV] = (ln * jax.nn.sigmoid(ln)).astype(bf16)

    ca_ref[HALO_A:HALO_A + rows, :] = z_ref[1] * z_ref[2]
    conv_a = caw_ref[K_SHORT - 1:K_SHORT, :] * ca_ref[HALO_A:HALO_A + rows, :]
    for j in range(K_SHORT - 1):
        shift = K_SHORT - 1 - j
        conv_a += caw_ref[j:j + 1, :] * ca_ref[HALO_A - shift:HALO_A - shift + rows, :]
    y_ref[:, 0:D_CONV] = (z_ref[0] * conv_a).astype(bf16)

    for c0 in range(0, D_MODEL, MXU_COLS):
        o_ref[:, c0:c0 + MXU_COLS] = h_ref[:, c0:c0 + MXU_COLS] + jnp.dot(
            y_ref[...], wout_ref[:, c0:c0 + MXU_COLS], preferred_element_type=f32)

    ca_ref[0:HALO_A, :] = ca_ref[rows:rows + HALO_A, :]
    cb_ref[0:HALO_B, :] = cb_ref[rows:span, :]


def _mixer(h, norm_g, w_in, conv_a_w, conv_b_w, conv_b_bias, ln_g, ln_b, w_out):
    row = lambda i: (i, 0)
    return pl.pallas_call(
        _mixer_kernel,
        grid=(SEQ // MIX_ROWS,),
        in_specs=[
            pl.BlockSpec(memory_space=pltpu.SMEM),
            pl.BlockSpec((MIX_ROWS, D_MODEL), row),
            _resident((1, D_MODEL)),
            _resident(w_in.shape),
            _resident(conv_a_w.shape),
            _resident(conv_b_w.shape),
            _resident((1, D_CONV)),
            _resident((1, D_CONV)),
            _resident((1, D_CONV)),
            _resident(w_out.shape),
        ],
        out_specs=pl.BlockSpec((MIX_ROWS, D_MODEL), row),
        out_shape=jax.ShapeDtypeStruct((SEQ, D_MODEL), f32),
        scratch_shapes=[
            pltpu.VMEM((MIX_ROWS, D_MODEL), bf16),
            pltpu.VMEM((3, MIX_ROWS, D_CONV), f32),
            pltpu.VMEM((HALO_A + MIX_ROWS, D_CONV), f32),
            pltpu.VMEM((SUBLANES * (HALO_B + MIX_ROWS) + MIX_ROWS, D_CONV), f32),
            pltpu.VMEM((MIX_ROWS, 2 * D_CONV), bf16),
        ],
        compiler_params=pltpu.CompilerParams(
            dimension_semantics=("arbitrary",),
            vmem_limit_bytes=VMEM_LIMIT_BYTES),
        name="conv_mixer",
    )(jnp.zeros((1,), jnp.int32), h, norm_g, w_in, conv_a_w, conv_b_w, conv_b_bias, ln_g, ln_b, w_out)


def _mem_fold_kernel(mem_ref, g_ref, wq_ref, wk_ref, wv_ref, wo_ref, wqk_ref, vo_ref):
    mem_n = _rmsnorm(mem_ref[...], g_ref[...]).astype(bf16)
    k = jnp.dot(mem_n, wk_ref[...].astype(bf16), preferred_element_type=f32).astype(bf16)
    v = jnp.dot(mem_n, wv_ref[...].astype(bf16), preferred_element_type=f32).astype(bf16)
    wqk_ref[...] = lax.dot_general(wq_ref[...].astype(bf16), k, (((1,), (1,)), ((), ())),
                                   preferred_element_type=f32).astype(bf16)
    vo_ref[...] = jnp.dot(v, wo_ref[...].astype(bf16), preferred_element_type=f32).astype(bf16)


def _mem_fold(mem, norm_g, w_q, w_kv, w_o):
    hd = MEM_HEAD_DIM
    return pl.pallas_call(
        _mem_fold_kernel,
        grid=(N_MEM_HEADS,),
        in_specs=[
            pl.BlockSpec((MEM_LEN, D_MODEL), lambda h: (0, 0)),
            pl.BlockSpec((1, D_MODEL), lambda h: (0, 0)),
            pl.BlockSpec((D_MODEL, hd), lambda h: (0, h)),
            pl.BlockSpec((D_MODEL, hd), lambda h: (0, h)),
            pl.BlockSpec((D_MODEL, hd), lambda h: (0, h + N_MEM_HEADS)),
            pl.BlockSpec((hd, D_MODEL), lambda h: (h, 0)),
        ],
        out_specs=(pl.BlockSpec((D_MODEL, MEM_LEN), lambda h: (0, h)),
                   pl.BlockSpec((MEM_LEN, D_MODEL), lambda h: (h, 0))),
        out_shape=(jax.ShapeDtypeStruct((D_MODEL, N_MEM_HEADS * MEM_LEN), bf16),
                   jax.ShapeDtypeStruct((N_MEM_HEADS * MEM_LEN, D_MODEL), bf16)),
        compiler_params=pltpu.CompilerParams(
            dimension_semantics=("arbitrary",),
            vmem_limit_bytes=VMEM_LIMIT_BYTES),
        name="mem_fold",
    )(mem, norm_g, w_q, w_kv, w_kv, w_o)


def _attn_kernel(h_ref, g_ref, wqk_ref, vo_ref, o_ref, u_ref, p_ref):
    x = h_ref[...]
    u_ref[...] = _rmsnorm(x, g_ref[...]).astype(bf16)
    for hd in range(N_MEM_HEADS):
        cols = slice(hd * MEM_LEN, (hd + 1) * MEM_LEN)
        s = jnp.dot(u_ref[...], wqk_ref[:, cols], preferred_element_type=f32) * (MEM_HEAD_DIM ** -0.5)
        e = jnp.exp(s - jnp.max(s, axis=-1, keepdims=True))
        p_ref[:, cols] = (e / jnp.sum(e, axis=-1, keepdims=True)).astype(bf16)
    o_ref[...] = x + jnp.dot(p_ref[...], vo_ref[...], preferred_element_type=f32)


def _attn(h, norm_g, w_qk, vo):
    row = lambda i: (i, 0)
    return pl.pallas_call(
        _attn_kernel,
        grid=(SEQ // ATT_ROWS,),
        in_specs=[
            pl.BlockSpec((ATT_ROWS, D_MODEL), row),
            _resident((1, D_MODEL)),
            _resident(w_qk.shape),
            _resident(vo.shape),
        ],
        out_specs=pl.BlockSpec((ATT_ROWS, D_MODEL), row),
        out_shape=jax.ShapeDtypeStruct((SEQ, D_MODEL), f32),
        scratch_shapes=[pltpu.VMEM((ATT_ROWS, D_MODEL), bf16),
                        pltpu.VMEM((ATT_ROWS, N_MEM_HEADS * MEM_LEN), bf16)],
        compiler_params=pltpu.CompilerParams(
            dimension_semantics=("arbitrary",),
            vmem_limit_bytes=VMEM_LIMIT_BYTES),
        name="mem_attn",
    )(h, norm_g, w_qk, vo)


def kernel(x, mem, ffn1_norm, ffn1_w_gate_up, ffn1_w_down, mix_norm, w_mix_in, conv_a_w, conv_b_w,
           conv_b_bias, ln_b_gain, ln_b_bias, w_mix_out, mem_q_norm, mem_kv_norm, w_mem_q, w_mem_kv,
           w_mem_o, ffn2_norm, ffn2_w_gate_up, ffn2_w_down, final_norm):
    assert x.shape == (1, SEQ, D_MODEL) and mem.shape == (1, MEM_LEN, D_MODEL)
    assert ffn1_norm.shape[0] == 1, "one layer"
    h = x[0]
    final_g = final_norm[None, :]

    riders = _cast_riders(ffn2_w_gate_up[0], ffn2_w_down[0], w_mix_in[0], w_mix_out[0])
    h, (w2_gate_up, w2_down, w_in, w_out) = _ffn(
        h, ffn1_norm, ffn1_w_gate_up[0].astype(bf16), ffn1_w_down[0].astype(bf16), final_g,
        final_norm=False, riders=riders)
    h = _mixer(h, mix_norm, w_in, conv_a_w[0], conv_b_w[0], conv_b_bias, ln_b_gain, ln_b_bias, w_out)
    w_qk, vo = _mem_fold(mem[0], mem_kv_norm, w_mem_q[0], w_mem_kv[0], w_mem_o[0])
    h = _attn(h, mem_q_norm, w_qk, vo)
    h, _ = _ffn(h, ffn2_norm, w2_gate_up, w2_down, final_g, final_norm=True)
    return h[None]
```

```python
import functools

import jax
import jax.numpy as jnp
from jax import lax
from jax.experimental import pallas as pl
from jax.experimental.pallas import tpu as pltpu

D_MODEL = 2048
SEQ = 16384
MEM_LEN = 256
D_FF = 5632
FFN_RES = 0.5
D_CONV = 1024
K_SHORT = 3
K_CONF = 31
N_MEM_HEADS = 4
MEM_HEAD_DIM = 512
RMS_EPS = 1e-6
LN_EPS = 1e-5

SUBLANES = 8
LANES = 128
MXU_COLS = 512
VMEM_LIMIT_BYTES = 60 * 1024 * 1024

FFN_ROWS = 1024
FFN_COLS = 512
MIX_ROWS = 256
ATT_ROWS = 1024
HALO_A = SUBLANES
HALO_B = 32
CONV_ROW_CHUNKS = (32,) * 8
assert sum(CONV_ROW_CHUNKS) == MIX_ROWS and all(c % SUBLANES == 0 for c in CONV_ROW_CHUNKS)

f32 = jnp.float32
bf16 = jnp.bfloat16


def _rmsnorm(x, g):
    return x * lax.rsqrt(jnp.mean(x * x, axis=-1, keepdims=True) + RMS_EPS) * g


def _resident(shape):
    return pl.BlockSpec(shape, lambda *_: (0,) * len(shape), pipeline_mode=pl.Buffered(1))


def _ffn_kernel(x_ref, g_ref, wg_ref, wu_ref, wd_ref, fin_ref, *rest, final_norm, n_riders):
    cast_in, (o_ref, *cast_out), (u_ref,) = rest[:n_riders], rest[n_riders:-1], rest[-1:]
    j = pl.program_id(1)
    last = pl.num_programs(1) - 1

    def step(first, final):
        if first:
            u_ref[...] = _rmsnorm(x_ref[...], g_ref[...]).astype(bf16)
        u = u_ref[...]
        halves = []
        for c0 in range(0, FFN_COLS, FFN_COLS // 2):
            cc = slice(c0, c0 + FFN_COLS // 2)
            gate = jnp.dot(u, wg_ref[:, cc], preferred_element_type=f32)
            up = jnp.dot(u, wu_ref[:, cc], preferred_element_type=f32)
            halves.append((gate * jax.nn.sigmoid(gate) * (FFN_RES * up)).astype(bf16))
        act = jnp.concatenate(halves, axis=1)
        acc = (x_ref if first else o_ref)[...] + jnp.dot(act, wd_ref[...], preferred_element_type=f32)
        o_ref[...] = _rmsnorm(acc, fin_ref[...]) if final else acc

        for src, dst in zip(cast_in, cast_out):
            if len(dst.shape) == 3:
                for b in range(dst.shape[0]):
                    dst[b] = src[:, b * FFN_COLS:(b + 1) * FFN_COLS].astype(bf16)
            else:
                dst[...] = src[...].astype(bf16)

    pl.when(j == 0)(functools.partial(step, True, False))
    if final_norm:
        pl.when((j > 0) & (j < last))(functools.partial(step, False, False))
        pl.when(j == last)(functools.partial(step, False, True))
    else:
        pl.when(j > 0)(functools.partial(step, False, False))


def _gate_up_specs(w_gate_up):
    n_col = D_FF // FFN_COLS
    if w_gate_up.ndim == 3:
        return (pl.BlockSpec((None, D_MODEL, FFN_COLS), lambda i, j: (j, 0, 0)),
                pl.BlockSpec((None, D_MODEL, FFN_COLS), lambda i, j: (j + n_col, 0, 0)))
    return (pl.BlockSpec((D_MODEL, FFN_COLS), lambda i, j: (0, j)),
            pl.BlockSpec((D_MODEL, FFN_COLS), lambda i, j: (0, j + n_col)))


def _cast_riders(ffn_w_gate_up, ffn_w_down, w_mix_in, w_mix_out):
    n_row, n_col = SEQ // FFN_ROWS, D_FF // FFN_COLS
    gu_rows, gu_cols = D_MODEL // n_row, 2 * D_FF // n_col
    dn_rows = D_FF // (n_row * n_col)
    mi_cols, mo_cols = MXU_COLS, MXU_COLS // 2
    mi_steps, mo_steps = w_mix_in.shape[1] // mi_cols, w_mix_out.shape[1] // mo_cols
    assert gu_cols == 2 * FFN_COLS and dn_rows % 16 == 0 and mi_steps <= n_col and mo_steps <= n_col
    srcs = [ffn_w_gate_up, ffn_w_down, w_mix_in, w_mix_out]
    in_specs = [
        pl.BlockSpec((gu_rows, gu_cols), lambda i, j: (i, j)),
        pl.BlockSpec((dn_rows, D_MODEL), lambda i, j: (i * n_col + j, 0)),
        pl.BlockSpec((gu_rows, mi_cols), lambda i, j: (i, jnp.minimum(j, mi_steps - 1))),
        pl.BlockSpec((gu_rows, mo_cols), lambda i, j: (i, jnp.minimum(j, mo_steps - 1))),
    ]
    out_shapes = [
        jax.ShapeDtypeStruct((2 * n_col, D_MODEL, FFN_COLS), bf16),
        jax.ShapeDtypeStruct(ffn_w_down.shape, bf16),
        jax.ShapeDtypeStruct(w_mix_in.shape, bf16),
        jax.ShapeDtypeStruct(w_mix_out.shape, bf16),
    ]
    out_specs = [
        pl.BlockSpec((2, gu_rows, FFN_COLS), lambda i, j: (j, i, 0)),
        in_specs[1], in_specs[2], in_specs[3],
    ]
    return srcs, in_specs, out_shapes, out_specs


def _ffn(h, norm_g, w_gate_up, w_down, final_g, *, final_norm, riders=None):
    srcs, rider_in, rider_shapes, rider_out = riders if riders else ([], [], [], [])
    gate_spec, up_spec = _gate_up_specs(w_gate_up)
    outs = pl.pallas_call(
        functools.partial(_ffn_kernel, final_norm=final_norm, n_riders=len(srcs)),
        grid=(SEQ // FFN_ROWS, D_FF // FFN_COLS),
        in_specs=[
            pl.BlockSpec((FFN_ROWS, D_MODEL), lambda i, j: (i, 0)),
            pl.BlockSpec((1, D_MODEL), lambda i, j: (0, 0)),
            gate_spec,
            up_spec,
            pl.BlockSpec((FFN_COLS, D_MODEL), lambda i, j: (j, 0)),
            pl.BlockSpec((1, D_MODEL), lambda i, j: (0, 0)),
            *rider_in,
        ],
        out_specs=[pl.BlockSpec((FFN_ROWS, D_MODEL), lambda i, j: (i, 0)), *rider_out],
        out_shape=[jax.ShapeDtypeStruct((SEQ, D_MODEL), f32), *rider_shapes],
        scratch_shapes=[pltpu.VMEM((FFN_ROWS, D_MODEL), bf16)],
        compiler_params=pltpu.CompilerParams(
            dimension_semantics=("arbitrary", "arbitrary"),
            vmem_limit_bytes=VMEM_LIMIT_BYTES),
        name="ffn_final" if final_norm else "ffn",
    )(h, norm_g, w_gate_up, w_gate_up, w_down, final_g, *srcs)
    return outs[0], outs[1:]


def _mixer_kernel(zero_ref, h_ref, g_ref, win_ref, caw_ref, cbw_ref, cbb_ref, lng_ref, lnb_ref,
                  wout_ref, o_ref, u_ref, z_ref, ca_ref, cb_ref, y_ref):
    rows = h_ref.shape[0]
    span = HALO_B + rows
    cv0 = SUBLANES * span

    @pl.when(pl.program_id(0) == 0)
    def _():
        ca_ref[0:HALO_A, :] = jnp.zeros((HALO_A, D_CONV), f32)
        cb_ref[0:HALO_B, :] = jnp.zeros((HALO_B, D_CONV), f32)

    x = h_ref[...]
    u_ref[...] = _rmsnorm(x, g_ref[...]).astype(bf16)

    def proj(col0, width=MXU_COLS):
        return jnp.dot(u_ref[...], win_ref[:, col0:col0 + width], preferred_element_type=f32)

    half = MXU_COLS // 2
    for c0 in range(0, D_CONV, half):
        cb_ref[HALO_B:span, c0:c0 + half] = (
            proj(3 * D_CONV + c0, half) * jax.nn.sigmoid(proj(4 * D_CONV + c0, half)))
    for k in range(3):
        for c0 in range(0, D_CONV, MXU_COLS):
            z_ref[k, :, c0:c0 + MXU_COLS] = proj(k * D_CONV + c0)

    for r in range(1, SUBLANES):
        cb_ref[r * span + SUBLANES:(r + 1) * span, :] = cb_ref[SUBLANES - r:span - r, :]
    hidden0 = pl.multiple_of(zero_ref[0], SUBLANES)
    for c in range(D_CONV // LANES):
        cols = slice(c * LANES, (c + 1) * LANES)
        taps = [cbw_ref[j:j + 1, cols] for j in range(K_CONF)]
        bias = cbb_ref[:, cols]
        row0 = 0
        for chunk in CONV_ROW_CHUNKS:
            acc = jnp.broadcast_to(bias, (chunk, LANES))
            for j in range(K_CONF):
                q, r = divmod(K_CONF - 1 - j, SUBLANES)
                start = r * span + HALO_B + row0 - q * SUBLANES
                acc = acc + taps[j] * cb_ref[start:start + chunk, cols]
            cb_ref[pl.ds(hidden0 + cv0 + row0, chunk), cols] = acc
            row0 += chunk
    cv = cb_ref[cv0:cv0 + rows, :]
    mu = jnp.mean(cv, axis=-1, keepdims=True)
    cen = cv - mu
    var = jnp.mean(cen * cen, axis=-1, keepdims=True)
    ln = cen * lax.rsqrt(var + LN_EPS) * lng_ref[...] + lnb_ref[...]
    y_ref[:, D_CONV:2 * D_CONV] = (ln * jax.nn.sigmoid(ln)).astype(bf16)

    ca_ref[HALO_A:HALO_A + rows, :] = z_ref[1] * z_ref[2]
    conv_a = caw_ref[K_SHORT - 1:K_SHORT, :] * ca_ref[HALO_A:HALO_A + rows, :]
    for j in range(K_SHORT - 1):
        shift = K_SHORT - 1 - j
        conv_a += caw_ref[j:j + 1, :] * ca_ref[HALO_A - shift:HALO_A - shift + rows, :]
    y_ref[:, 0:D_CONV] = (z_ref[0] * conv_a).astype(bf16)

    for c0 in range(0, D_MODEL, MXU_COLS):
        o_ref[:, c0:c0 + MXU_COLS] = h_ref[:, c0:c0 + MXU_COLS] + jnp.dot(
            y_ref[...], wout_ref[:, c0:c0 + MXU_COLS], preferred_element_type=f32)

    ca_ref[0:HALO_A, :] = ca_ref[rows:rows + HALO_A, :]
    cb_ref[0:HALO_B, :] = cb_ref[rows:span, :]


def _mixer(h, norm_g, w_in, conv_a_w, conv_b_w, conv_b_bias, ln_g, ln_b, w_out):
    row = lambda i: (i, 0)
    return pl.pallas_call(
        _mixer_kernel,
        grid=(SEQ // MIX_ROWS,),
        in_specs=[
            pl.BlockSpec(memory_space=pltpu.SMEM),
            pl.BlockSpec((MIX_ROWS, D_MODEL), row),
            _resident((1, D_MODEL)),
            _resident(w_in.shape),
            _resident(conv_a_w.shape),
            _resident(conv_b_w.shape),
            _resident((1, D_CONV)),
            _resident((1, D_CONV)),
            _resident((1, D_CONV)),
            _resident(w_out.shape),
        ],
        out_specs=pl.BlockSpec((MIX_ROWS, D_MODEL), row),
        out_shape=jax.ShapeDtypeStruct((SEQ, D_MODEL), f32),
        scratch_shapes=[
            pltpu.VMEM((MIX_ROWS, D_MODEL), bf16),
            pltpu.VMEM((3, MIX_ROWS, D_CONV), f32),
            pltpu.VMEM((HALO_A + MIX_ROWS, D_CONV), f32),
            pltpu.VMEM((SUBLANES * (HALO_B + MIX_ROWS) + MIX_ROWS, D_CONV), f32),
            pltpu.VMEM((MIX_ROWS, 2 * D_CONV), bf16),
        ],
        compiler_params=pltpu.CompilerParams(
            dimension_semantics=("arbitrary",),
            vmem_limit_bytes=VMEM_LIMIT_BYTES),
        name="conv_mixer",
    )(jnp.zeros((1,), jnp.int32), h, norm_g, w_in, conv_a_w, conv_b_w, conv_b_bias, ln_g, ln_b, w_out)


def _mem_fold_kernel(mem_ref, g_ref, wq_ref, wk_ref, wv_ref, wo_ref, wqk_ref, vo_ref):
    mem_n = _rmsnorm(mem_ref[...], g_ref[...]).astype(bf16)
    k = jnp.dot(mem_n, wk_ref[...].astype(bf16), preferred_element_type=f32).astype(bf16)
    v = jnp.dot(mem_n, wv_ref[...].astype(bf16), preferred_element_type=f32).astype(bf16)
    wqk_ref[...] = lax.dot_general(wq_ref[...].astype(bf16), k, (((1,), (1,)), ((), ())),
                                   preferred_element_type=f32).astype(bf16)
    vo_ref[...] = jnp.dot(v, wo_ref[...].astype(bf16), preferred_element_type=f32).astype(bf16)


def _mem_fold(mem, norm_g, w_q, w_kv, w_o):
    hd = MEM_HEAD_DIM
    return pl.pallas_call(
        _mem_fold_kernel,
        grid=(N_MEM_HEADS,),
        in_specs=[
            pl.BlockSpec((MEM_LEN, D_MODEL), lambda h: (0, 0)),
            pl.BlockSpec((1, D_MODEL), lambda h: (0, 0)),
            pl.BlockSpec((D_MODEL, hd), lambda h: (0, h)),
            pl.BlockSpec((D_MODEL, hd), lambda h: (0, h)),
            pl.BlockSpec((D_MODEL, hd), lambda h: (0, h + N_MEM_HEADS)),
            pl.BlockSpec((hd, D_MODEL), lambda h: (h, 0)),
        ],
        out_specs=(pl.BlockSpec((D_MODEL, MEM_LEN), lambda h: (0, h)),
                   pl.BlockSpec((MEM_LEN, D_MODEL), lambda h: (h, 0))),
        out_shape=(jax.ShapeDtypeStruct((D_MODEL, N_MEM_HEADS * MEM_LEN), bf16),
                   jax.ShapeDtypeStruct((N_MEM_HEADS * MEM_LEN, D_MODEL), bf16)),
        compiler_params=pltpu.CompilerParams(
            dimension_semantics=("arbitrary",),
            vmem_limit_bytes=VMEM_LIMIT_BYTES),
        name="mem_fold",
    )(mem, norm_g, w_q, w_kv, w_kv, w_o)


def _attn_kernel(h_ref, g_ref, wqk_ref, vo_ref, o_ref, u_ref, p_ref):
    x = h_ref[...]
    u_ref[...] = _rmsnorm(x, g_ref[...]).astype(bf16)
    for hd in range(N_MEM_HEADS):
        cols = slice(hd * MEM_LEN, (hd + 1) * MEM_LEN)
        s = jnp.dot(u_ref[...], wqk_ref[:, cols], preferred_element_type=f32) * (MEM_HEAD_DIM ** -0.5)
        e = jnp.exp(s - jnp.max(s, axis=-1, keepdims=True))
        p_ref[:, cols] = (e / jnp.sum(e, axis=-1, keepdims=True)).astype(bf16)
    o_ref[...] = x + jnp.dot(p_ref[...], vo_ref[...], preferred_element_type=f32)


def _attn(h, norm_g, w_qk, vo):
    row = lambda i: (i, 0)
    return pl.pallas_call(
        _attn_kernel,
        grid=(SEQ // ATT_ROWS,),
        in_specs=[
            pl.BlockSpec((ATT_ROWS, D_MODEL), row),
            _resident((1, D_MODEL)),
            _resident(w_qk.shape),
            _resident(vo.shape),
        ],
        out_specs=pl.BlockSpec((ATT_ROWS, D_MODEL), row),
        out_shape=jax.ShapeDtypeStruct((SEQ, D_MODEL), f32),
        scratch_shapes=[pltpu.VMEM((ATT_ROWS, D_MODEL), bf16),
                        pltpu.VMEM((ATT_ROWS, N_MEM_HEADS * MEM_LEN), bf16)],
        compiler_params=pltpu.CompilerParams(
            dimension_semantics=("arbitrary",),
            vmem_limit_bytes=VMEM_LIMIT_BYTES),
        name="mem_attn",
    )(h, norm_g, w_qk, vo)


def kernel(x, mem, ffn1_norm, ffn1_w_gate_up, ffn1_w_down, mix_norm, w_mix_in, conv_a_w, conv_b_w,
           conv_b_bias, ln_b_gain, ln_b_bias, w_mix_out, mem_q_norm, mem_kv_norm, w_mem_q, w_mem_kv,
           w_mem_o, ffn2_norm, ffn2_w_gate_up, ffn2_w_down, final_norm):
    assert x.shape == (1, SEQ, D_MODEL) and mem.shape == (1, MEM_LEN, D_MODEL)
    assert ffn1_norm.shape[0] == 1, "one layer"
    h = x[0]
    final_g = final_norm[None, :]

    riders = _cast_riders(ffn2_w_gate_up[0], ffn2_w_down[0], w_mix_in[0], w_mix_out[0])
    h, (w2_gate_up, w2_down, w_in, w_out) = _ffn(
        h, ffn1_norm, ffn1_w_gate_up[0].astype(bf16), ffn1_w_down[0].astype(bf16), final_g,
        final_norm=False, riders=riders)
    h = _mixer(h, mix_norm, w_in, conv_a_w[0], conv_b_w[0], conv_b_bias, ln_b_gain, ln_b_bias, w_out)
    w_qk, vo = _mem_fold(mem[0], mem_kv_norm, w_mem_q[0], w_mem_kv[0], w_mem_o[0])
    h = _attn(h, mem_q_norm, w_qk, vo)
    h, _ = _ffn(h, ffn2_norm, w2_gate_up, w2_down, final_g, final_norm=True)
    return h[None]
```

```python
import functools

import jax
import jax.numpy as jnp
from jax import lax
from jax.experimental import pallas as pl
from jax.experimental.pallas import tpu as pltpu

D_MODEL = 2048
SEQ = 16384
MEM_LEN = 256
D_FF = 5632
FFN_RES = 0.5
D_CONV = 1024
K_SHORT = 3
K_CONF = 31
N_MEM_HEADS = 4
MEM_HEAD_DIM = 512
RMS_EPS = 1e-6
LN_EPS = 1e-5

SUBLANES = 8
LANES = 128
MXU_COLS = 512
VMEM_LIMIT_BYTES = 60 * 1024 * 1024

FFN_ROWS = 1024
FFN_COLS = 512
MIX_ROWS = 256
ATT_ROWS = 1024
HALO_A = SUBLANES
HALO_B = 32
CONV_ROW_CHUNKS = (32,) * 8
assert sum(CONV_ROW_CHUNKS) == MIX_ROWS and all(c % SUBLANES == 0 for c in CONV_ROW_CHUNKS)

f32 = jnp.float32
bf16 = jnp.bfloat16


def _rmsnorm(x, g):
    return x * lax.rsqrt(jnp.mean(x * x, axis=-1, keepdims=True) + RMS_EPS) * g


def _resident(shape):
    return pl.BlockSpec(shape, lambda *_: (0,) * len(shape), pipeline_mode=pl.Buffered(1))


def _ffn_kernel(x_ref, g_ref, wg_ref, wu_ref, wd_ref, fin_ref, *rest, final_norm, n_riders):
    cast_in, (o_ref, *cast_out), (u_ref,) = rest[:n_riders], rest[n_riders:-1], rest[-1:]
    j = pl.program_id(1)
    last = pl.num_programs(1) - 1

    def step(first, final):
        if first:
            u_ref[...] = _rmsnorm(x_ref[...], g_ref[...]).astype(bf16)
        u = u_ref[...]
        halves = []
        for c0 in range(0, FFN_COLS, FFN_COLS // 2):
            cc = slice(c0, c0 + FFN_COLS // 2)
            gate = jnp.dot(u, wg_ref[:, cc], preferred_element_type=f32)
            up = jnp.dot(u, wu_ref[:, cc], preferred_element_type=f32)
            halves.append((gate * jax.nn.sigmoid(gate) * (FFN_RES * up)).astype(bf16))
        act = jnp.concatenate(halves, axis=1)
        base_ref = x_ref if first else o_ref
        for r0 in range(0, FFN_ROWS, FFN_ROWS // 2):
            rr = slice(r0, r0 + FFN_ROWS // 2)
            acc = base_ref[rr, :] + jnp.dot(act[rr, :], wd_ref[...], preferred_element_type=f32)
            o_ref[rr, :] = _rmsnorm(acc, fin_ref[...]) if final else acc

        for src, dst in zip(cast_in, cast_out):
            if len(dst.shape) == 3:
                for b in range(dst.shape[0]):
                    dst[b] = src[:, b * FFN_COLS:(b + 1) * FFN_COLS].astype(bf16)
            else:
                dst[...] = src[...].astype(bf16)

    pl.when(j == 0)(functools.partial(step, True, False))
    if final_norm:
        pl.when((j > 0) & (j < last))(functools.partial(step, False, False))
        pl.when(j == last)(functools.partial(step, False, True))
    else:
        pl.when(j > 0)(functools.partial(step, False, False))


def _gate_up_specs(w_gate_up):
    n_col = D_FF // FFN_COLS
    if w_gate_up.ndim == 3:
        return (pl.BlockSpec((None, D_MODEL, FFN_COLS), lambda i, j: (j, 0, 0)),
                pl.BlockSpec((None, D_MODEL, FFN_COLS), lambda i, j: (j + n_col, 0, 0)))
    return (pl.BlockSpec((D_MODEL, FFN_COLS), lambda i, j: (0, j)),
            pl.BlockSpec((D_MODEL, FFN_COLS), lambda i, j: (0, j + n_col)))


def _cast_riders(ffn_w_gate_up, ffn_w_down, w_mix_in, w_mix_out):
    n_row, n_col = SEQ // FFN_ROWS, D_FF // FFN_COLS
    gu_rows, gu_cols = D_MODEL // n_row, 2 * D_FF // n_col
    dn_rows = D_FF // (n_row * n_col)
    mi_cols, mo_cols = MXU_COLS, MXU_COLS // 2
    mi_steps, mo_steps = w_mix_in.shape[1] // mi_cols, w_mix_out.shape[1] // mo_cols
    assert gu_cols == 2 * FFN_COLS and dn_rows % 16 == 0 and mi_steps <= n_col and mo_steps <= n_col
    srcs = [ffn_w_gate_up, ffn_w_down, w_mix_in, w_mix_out]
    in_specs = [
        pl.BlockSpec((gu_rows, gu_cols), lambda i, j: (i, j)),
        pl.BlockSpec((dn_rows, D_MODEL), lambda i, j: (i * n_col + j, 0)),
        pl.BlockSpec((gu_rows, mi_cols), lambda i, j: (i, jnp.minimum(j, mi_steps - 1))),
        pl.BlockSpec((gu_rows, mo_cols), lambda i, j: (i, jnp.minimum(j, mo_steps - 1))),
    ]
    out_shapes = [
        jax.ShapeDtypeStruct((2 * n_col, D_MODEL, FFN_COLS), bf16),
        jax.ShapeDtypeStruct(ffn_w_down.shape, bf16),
        jax.ShapeDtypeStruct(w_mix_in.shape, bf16),
        jax.ShapeDtypeStruct(w_mix_out.shape, bf16),
    ]
    out_specs = [
        pl.BlockSpec((2, gu_rows, FFN_COLS), lambda i, j: (j, i, 0)),
        in_specs[1], in_specs[2], in_specs[3],
    ]
    return srcs, in_specs, out_shapes, out_specs


def _ffn(h, norm_g, w_gate_up, w_down, final_g, *, final_norm, riders=None):
    srcs, rider_in, rider_shapes, rider_out = riders if riders else ([], [], [], [])
    gate_spec, up_spec = _gate_up_specs(w_gate_up)
    outs = pl.pallas_call(
        functools.partial(_ffn_kernel, final_norm=final_norm, n_riders=len(srcs)),
        grid=(SEQ // FFN_ROWS, D_FF // FFN_COLS),
        in_specs=[
            pl.BlockSpec((FFN_ROWS, D_MODEL), lambda i, j: (i, 0)),
            pl.BlockSpec((1, D_MODEL), lambda i, j: (0, 0)),
            gate_spec,
            up_spec,
            pl.BlockSpec((FFN_COLS, D_MODEL), lambda i, j: (j, 0)),
            pl.BlockSpec((1, D_MODEL), lambda i, j: (0, 0)),
            *rider_in,
        ],
        out_specs=[pl.BlockSpec((FFN_ROWS, D_MODEL), lambda i, j: (i, 0)), *rider_out],
        out_shape=[jax.ShapeDtypeStruct((SEQ, D_MODEL), f32), *rider_shapes],
        scratch_shapes=[pltpu.VMEM((FFN_ROWS, D_MODEL), bf16)],
        compiler_params=pltpu.CompilerParams(
            dimension_semantics=("arbitrary", "arbitrary"),
            vmem_limit_bytes=VMEM_LIMIT_BYTES),
        name="ffn_final" if final_norm else "ffn",
    )(h, norm_g, w_gate_up, w_gate_up, w_down, final_g, *srcs)
    return outs[0], outs[1:]


def _mixer_kernel(zero_ref, h_ref, g_ref, win_ref, caw_ref, cbw_ref, cbb_ref, lng_ref, lnb_ref,
                  wout_ref, o_ref, u_ref, z_ref, ca_ref, cb_ref, y_ref):
    rows = h_ref.shape[0]
    span = HALO_B + rows
    cv0 = SUBLANES * span

    @pl.when(pl.program_id(0) == 0)
    def _():
        ca_ref[0:HALO_A, :] = jnp.zeros((HALO_A, D_CONV), f32)
        cb_ref[0:HALO_B, :] = jnp.zeros((HALO_B, D_CONV), f32)

    x = h_ref[...]
    u_ref[...] = _rmsnorm(x, g_ref[...]).astype(bf16)

    def proj(col0):
        return jnp.dot(u_ref[...], win_ref[:, col0:col0 + MXU_COLS], preferred_element_type=f32)

    for c0 in range(0, D_CONV, MXU_COLS):
        cb_ref[HALO_B:span, c0:c0 + MXU_COLS] = (
            proj(3 * D_CONV + c0) * jax.nn.sigmoid(proj(4 * D_CONV + c0)))
    for k in range(3):
        for c0 in range(0, D_CONV, MXU_COLS):
            z_ref[k, :, c0:c0 + MXU_COLS] = proj(k * D_CONV + c0)

    for r in range(1, SUBLANES):
        cb_ref[r * span + SUBLANES:(r + 1) * span, :] = cb_ref[SUBLANES - r:span - r, :]
    hidden0 = pl.multiple_of(zero_ref[0], SUBLANES)
    for c in range(D_CONV // LANES):
        cols = slice(c * LANES, (c + 1) * LANES)
        taps = [cbw_ref[j:j + 1, cols] for j in range(K_CONF)]
        bias = cbb_ref[:, cols]
        row0 = 0
        for chunk in CONV_ROW_CHUNKS:
            acc = jnp.broadcast_to(bias, (chunk, LANES))
            for j in range(K_CONF):
                q, r = divmod(K_CONF - 1 - j, SUBLANES)
                start = r * span + HALO_B + row0 - q * SUBLANES
                acc = acc + taps[j] * cb_ref[start:start + chunk, cols]
            cb_ref[pl.ds(hidden0 + cv0 + row0, chunk), cols] = acc
            row0 += chunk
    cv = cb_ref[cv0:cv0 + rows, :]
    mu = jnp.mean(cv, axis=-1, keepdims=True)
    cen = cv - mu
    var = jnp.mean(cen * cen, axis=-1, keepdims=True)
    ln = cen * lax.rsqrt(var + LN_EPS) * lng_ref[...] + lnb_ref[...]
    y_ref[:, D_CONV:2 * D_CONV] = (ln * jax.nn.sigmoid(ln)).astype(bf16)

    ca_ref[HALO_A:HALO_A + rows, :] = z_ref[1] * z_ref[2]
    conv_a = caw_ref[K_SHORT - 1:K_SHORT, :] * ca_ref[HALO_A:HALO_A + rows, :]
    for j in range(K_SHORT - 1):
        shift = K_SHORT - 1 - j
        conv_a += caw_ref[j:j + 1, :] * ca_ref[HALO_A - shift:HALO_A - shift + rows, :]
    y_ref[:, 0:D_CONV] = (z_ref[0] * conv_a).astype(bf16)

    for c0 in range(0, D_MODEL, MXU_COLS):
        o_ref[:, c0:c0 + MXU_COLS] = h_ref[:, c0:c0 + MXU_COLS] + jnp.dot(
            y_ref[...], wout_ref[:, c0:c0 + MXU_COLS], preferred_element_type=f32)

    ca_ref[0:HALO_A, :] = ca_ref[rows:rows + HALO_A, :]
    cb_ref[0:HALO_B, :] = cb_ref[rows:span, :]


def _mixer(h, norm_g, w_in, conv_a_w, conv_b_w, conv_b_bias, ln_g, ln_b, w_out):
    row = lambda i: (i, 0)
    return pl.pallas_call(
        _mixer_kernel,
        grid=(SEQ // MIX_ROWS,),
        in_specs=[
            pl.BlockSpec(memory_space=pltpu.SMEM),
            pl.BlockSpec((MIX_ROWS, D_MODEL), row),
            _resident((1, D_MODEL)),
            _resident(w_in.shape),
            _resident(conv_a_w.shape),
            _resident(conv_b_w.shape),
            _resident((1, D_CONV)),
            _resident((1, D_CONV)),
            _resident((1, D_CONV)),
            _resident(w_out.shape),
        ],
        out_specs=pl.BlockSpec((MIX_ROWS, D_MODEL), row),
        out_shape=jax.ShapeDtypeStruct((SEQ, D_MODEL), f32),
        scratch_shapes=[
            pltpu.VMEM((MIX_ROWS, D_MODEL), bf16),
            pltpu.VMEM((3, MIX_ROWS, D_CONV), f32),
            pltpu.VMEM((HALO_A + MIX_ROWS, D_CONV), f32),
            pltpu.VMEM((SUBLANES * (HALO_B + MIX_ROWS) + MIX_ROWS, D_CONV), f32),
            pltpu.VMEM((MIX_ROWS, 2 * D_CONV), bf16),
        ],
        compiler_params=pltpu.CompilerParams(
            dimension_semantics=("arbitrary",),
            vmem_limit_bytes=VMEM_LIMIT_BYTES),
        name="conv_mixer",
    )(jnp.zeros((1,), jnp.int32), h, norm_g, w_in, conv_a_w, conv_b_w, conv_b_bias, ln_g, ln_b, w_out)


def _mem_fold_kernel(mem_ref, g_ref, wq_ref, wk_ref, wv_ref, wo_ref, wqk_ref, vo_ref):
    mem_n = _rmsnorm(mem_ref[...], g_ref[...]).astype(bf16)
    k = jnp.dot(mem_n, wk_ref[...].astype(bf16), preferred_element_type=f32).astype(bf16)
    v = jnp.dot(mem_n, wv_ref[...].astype(bf16), preferred_element_type=f32).astype(bf16)
    wqk_ref[...] = lax.dot_general(wq_ref[...].astype(bf16), k, (((1,), (1,)), ((), ())),
                                   preferred_element_type=f32).astype(bf16)
    vo_ref[...] = jnp.dot(v, wo_ref[...].astype(bf16), preferred_element_type=f32).astype(bf16)


def _mem_fold(mem, norm_g, w_q, w_kv, w_o):
    hd = MEM_HEAD_DIM
    return pl.pallas_call(
        _mem_fold_kernel,
        grid=(N_MEM_HEADS,),
        in_specs=[
            pl.BlockSpec((MEM_LEN, D_MODEL), lambda h: (0, 0)),
            pl.BlockSpec((1, D_MODEL), lambda h: (0, 0)),
            pl.BlockSpec((D_MODEL, hd), lambda h: (0, h)),
            pl.BlockSpec((D_MODEL, hd), lambda h: (0, h)),
            pl.BlockSpec((D_MODEL, hd), lambda h: (0, h + N_MEM_HEADS)),
            pl.BlockSpec((hd, D_MODEL), lambda h: (h, 0)),
        ],
        out_specs=(pl.BlockSpec((D_MODEL, MEM_LEN), lambda h: (0, h)),
                   pl.BlockSpec((MEM_LEN, D_MODEL), lambda h: (h, 0))),
        out_shape=(jax.ShapeDtypeStruct((D_MODEL, N_MEM_HEADS * MEM_LEN), bf16),
                   jax.ShapeDtypeStruct((N_MEM_HEADS * MEM_LEN, D_MODEL), bf16)),
        compiler_params=pltpu.CompilerParams(
            dimension_semantics=("arbitrary",),
            vmem_limit_bytes=VMEM_LIMIT_BYTES),
        name="mem_fold",
    )(mem, norm_g, w_q, w_kv, w_kv, w_o)


def _attn_kernel(h_ref, g_ref, wqk_ref, vo_ref, o_ref, u_ref, p_ref):
    x = h_ref[...]
    u_ref[...] = _rmsnorm(x, g_ref[...]).astype(bf16)
    for hd in range(N_MEM_HEADS):
        cols = slice(hd * MEM_LEN, (hd + 1) * MEM_LEN)
        s = jnp.dot(u_ref[...], wqk_ref[:, cols], preferred_element_type=f32) * (MEM_HEAD_DIM ** -0.5)
        e = jnp.exp(s - jnp.max(s, axis=-1, keepdims=True))
        p_ref[:, cols] = (e / jnp.sum(e, axis=-1, keepdims=True)).astype(bf16)
    o_ref[...] = x + jnp.dot(p_ref[...], vo_ref[...], preferred_element_type=f32)


def _attn(h, norm_g, w_qk, vo):
    row = lambda i: (i, 0)
    return pl.pallas_call(
        _attn_kernel,
        grid=(SEQ // ATT_ROWS,),
        in_specs=[
            pl.BlockSpec((ATT_ROWS, D_MODEL), row),
            _resident((1, D_MODEL)),
            _resident(w_qk.shape),
            _resident(vo.shape),
        ],
        out_specs=pl.BlockSpec((ATT_ROWS, D_MODEL), row),
        out_shape=jax.ShapeDtypeStruct((SEQ, D_MODEL), f32),
        scratch_shapes=[pltpu.VMEM((ATT_ROWS, D_MODEL), bf16),
                        pltpu.VMEM((ATT_ROWS, N_MEM_HEADS * MEM_LEN), bf16)],
        compiler_params=pltpu.CompilerParams(
            dimension_semantics=("arbitrary",),
            vmem_limit_bytes=VMEM_LIMIT_BYTES),
        name="mem_attn",
    )(h, norm_g, w_qk, vo)


def kernel(x, mem, ffn1_norm, ffn1_w_gate_up, ffn1_w_down, mix_norm, w_mix_in, conv_a_w, conv_b_w,
           conv_b_bias, ln_b_gain, ln_b_bias, w_mix_out, mem_q_norm, mem_kv_norm, w_mem_q, w_mem_kv,
           w_mem_o, ffn2_norm, ffn2_w_gate_up, ffn2_w_down, final_norm):
    assert x.shape == (1, SEQ, D_MODEL) and mem.shape == (1, MEM_LEN, D_MODEL)
    assert ffn1_norm.shape[0] == 1, "one layer"
    h = x[0]
    final_g = final_norm[None, :]

    riders = _cast_riders(ffn2_w_gate_up[0], ffn2_w_down[0], w_mix_in[0], w_mix_out[0])
    h, (w2_gate_up, w2_down, w_in, w_out) = _ffn(
        h, ffn1_norm, ffn1_w_gate_up[0].astype(bf16), ffn1_w_down[0].astype(bf16), final_g,
        final_norm=False, riders=riders)
    h = _mixer(h, mix_norm, w_in, conv_a_w[0], conv_b_w[0], conv_b_bias, ln_b_gain, ln_b_bias, w_out)
    w_qk, vo = _mem_fold(mem[0], mem_kv_norm, w_mem_q[0], w_mem_kv[0], w_mem_o[0])
    h = _attn(h, mem_q_norm, w_qk, vo)
    h, _ = _ffn(h, ffn2_norm, w2_gate_up, w2_down, final_g, final_norm=True)
    return h[None]
```
